```python
import math
import jax
import jax.numpy as jnp
from jax import lax
import numpy as np

D_MODEL = 1024
BATCH = 16
SEQ = 2048
DEPTH = 2
DEC_BATCH = 128
DEC_SEQ = 8
PAST_LEN = 16384
PAGE_SIZE = 128

PLE_DIM = 256
CONV_GROUPS = 4
CONV_GROUP_DIM = 64
CONV_WIDTH = CONV_GROUPS * CONV_GROUP_DIM
CONV_K = 3
MLA_HEADS = 6
MLA_Q_LORA = 256
MLA_KV_LORA = 128
MLA_NOPE = 64
MLA_ROPE = 32
MLA_V = 64
MLA_ROPE_THETA = 10000.0
MLA_Q_CHUNK = 128
MLA_PAGE_CHUNK = 16
MOBA_HEADS = 6
MOBA_KV_HEADS = 2
MOBA_HEAD_DIM = 64
MOBA_BLOCK = 256
MOBA_TOPK = 3
MOBA_ROT = MOBA_HEAD_DIM // 4
MOBA_Q_CHUNK = 16
ROPE_THETA = 500000.0
MIX_WIDTH = CONV_WIDTH + MLA_HEADS * MLA_V + MOBA_HEADS * MOBA_HEAD_DIM
N_EXPERTS = 32
TOP_K = 4
D_FF = 1024
SWIGLU_LIMIT = 7.0
SWIGLU_ALPHA = 1.702
MOE_BLOCK = 128
DEEPNORM_ALPHA = (2 * DEPTH) ** 0.25
DEEPNORM_BETA = (8 * DEPTH) ** -0.25
LN_EPS = 1e-5
RMS_EPS = 1e-6
NEG_INF = -1e30
MLA_SCALE = (MLA_NOPE + MLA_ROPE) ** -0.5
MOBA_SCALE = MOBA_HEAD_DIM ** -0.5
IN_SIZES = (CONV_WIDTH, CONV_WIDTH, CONV_WIDTH, MLA_Q_LORA, MLA_KV_LORA, MLA_ROPE,
            MOBA_HEADS * MOBA_HEAD_DIM, MOBA_KV_HEADS * MOBA_HEAD_DIM, MOBA_KV_HEADS * MOBA_HEAD_DIM)
IN_WIDTH = sum(IN_SIZES)
IN_OFFSETS = tuple(int(o) for o in np.cumsum(IN_SIZES)[:-1])

kernel_name = 'hymba_conv_mla_moba_moe_step'


def _layer_norm(x, g, b):
    xf = x.astype(jnp.float32)
    mu = jnp.mean(xf, -1, keepdims=True)
    xc = xf - mu
    var = jnp.mean(xc * xc, -1, keepdims=True)
    y = xc * lax.rsqrt(var + LN_EPS) * g.astype(jnp.float32) + b.astype(jnp.float32)
    return y.astype(x.dtype)


def _rms_norm(x, g):
    xf = x.astype(jnp.float32)
    y = xf * lax.rsqrt(jnp.mean(xf * xf, -1, keepdims=True) + RMS_EPS) * g.astype(jnp.float32)
    return y.astype(x.dtype)


def _rope_tables(pos, dim, theta):
    inv = theta ** (-jnp.arange(0, dim, 2, dtype=jnp.float32) / dim)
    ang = pos.astype(jnp.float32)[:, None] * inv[None, :]
    return jnp.cos(ang), jnp.sin(ang)


def _apply_rope(x, cos, sin):
    half = x.shape[-1] // 2
    xf = x.astype(jnp.float32)
    x1, x2 = xf[..., :half], xf[..., half:]
    c, s = cos[None, :, None, :], sin[None, :, None, :]
    return jnp.concatenate([x1 * c - x2 * s, x2 * c + x1 * s], -1).astype(x.dtype)


def _partial_rope(x, cos, sin):
    return jnp.concatenate([_apply_rope(x[..., :MOBA_ROT], cos, sin), x[..., MOBA_ROT:]], -1)


def _conv_mixer(h, b_gate, c_gate, prev, w):
    u = c_gate * h
    ext = jnp.concatenate([prev.astype(u.dtype), u], axis=1)
    y = lax.conv_general_dilated(ext, w[:, None, :].astype(u.dtype), window_strides=(1,),
                                 padding='VALID', dimension_numbers=('NWC', 'WIO', 'NWC'),
                                 feature_group_count=u.shape[-1])
    return b_gate * y, ext[:, -(CONV_K - 1):]


def _mixer_projections(x, pos, lw):
    B, S, _ = x.shape
    (cv_h, cv_b, cv_c, q_lat, kv_lat, k_pe, mq, mk, mv) = jnp.split(x @ lw['w_in'], list(IN_OFFSETS), axis=-1)
    q = (_rms_norm(q_lat, lw['mla_q_norm']) @ lw['mla_w_uq']).reshape(B, S, MLA_HEADS, MLA_NOPE + MLA_ROPE)
    cos, sin = _rope_tables(pos, MLA_ROPE, MLA_ROPE_THETA)
    q_pe = _apply_rope(q[..., MLA_NOPE:], cos, sin)
    q_abs = jnp.einsum('bshn,rhn->bshr', q[..., :MLA_NOPE], lw['mla_w_uk'])
    c_kv = _rms_norm(kv_lat, lw['mla_kv_norm'])
    k_pe = _apply_rope(k_pe[:, :, None, :], cos, sin)[:, :, 0]
    cos2, sin2 = _rope_tables(pos, MOBA_ROT, ROPE_THETA)
    mq = _partial_rope(mq.reshape(B, S, MOBA_HEADS, MOBA_HEAD_DIM), cos2, sin2)
    mk = _partial_rope(mk.reshape(B, S, MOBA_KV_HEADS, MOBA_HEAD_DIM), cos2, sin2)
    mv = mv.reshape(B, S, MOBA_KV_HEADS, MOBA_HEAD_DIM)
    return (cv_h, cv_b, cv_c), (q_abs, q_pe, c_kv, k_pe), (mq, mk, mv)


def _mla_scores(q_abs, q_pe, lat, rp):
    return (jnp.einsum('bqhr,bkr->bhqk', q_abs, lat, preferred_element_type=jnp.float32)
            + jnp.einsum('bqhp,bkp->bhqk', q_pe, rp, preferred_element_type=jnp.float32)) * MLA_SCALE


def _mla_prompt(q_abs, q_pe, c_kv, k_pe):
    B, S, H, R = q_abs.shape
    nq = S // MLA_Q_CHUNK
    kpos = jnp.arange(S)

    def split_q(a):
        return a.reshape(B, nq, MLA_Q_CHUNK, *a.shape[2:]).swapaxes(0, 1)

    def block(args):
        qa, qp, c = args
        s = _mla_scores(qa, qp, c_kv, k_pe)
        qpos = c * MLA_Q_CHUNK + jnp.arange(MLA_Q_CHUNK)
        s = jnp.where(kpos[None, :] <= qpos[:, None], s, NEG_INF)
        p = jax.nn.softmax(s, axis=-1).astype(c_kv.dtype)
        return jnp.einsum('bhqk,bkr->bqhr', p, c_kv)

    o = lax.map(block, (split_q(q_abs), split_q(q_pe), jnp.arange(nq)))
    return o.swapaxes(0, 1).reshape(B, S, H, R)


def _mla_sample(q_abs, q_pe, c_new, kpe_new, cache_lat, cache_rope, page_table, layer):
    B, T, H, R = q_abs.shape
    n_pages = page_table.shape[1]
    pc = math.gcd(n_pages, MLA_PAGE_CHUNK)
    pt = page_table.reshape(B, n_pages // pc, pc).swapaxes(0, 1)

    def step(carry, pt_c):
        m, l, acc = carry
        lat = cache_lat[layer, pt_c].reshape(B, pc * PAGE_SIZE, R).astype(q_abs.dtype)
        rp = cache_rope[layer, pt_c].reshape(B, pc * PAGE_SIZE, MLA_ROPE).astype(q_abs.dtype)
        s = _mla_scores(q_abs, q_pe, lat, rp)
        m_new = jnp.maximum(m, s.max(-1))
        w = jnp.exp(s - m_new[..., None])
        corr = jnp.exp(m - m_new)
        acc = acc * corr[..., None] + jnp.einsum('bhtk,bkr->bhtr', w, lat, preferred_element_type=jnp.float32)
        return (m_new, l * corr + w.sum(-1), acc), None

    init = (jnp.full((B, H, T), NEG_INF, jnp.float32), jnp.zeros((B, H, T), jnp.float32),
            jnp.zeros((B, H, T, R), jnp.float32))
    (m, l, acc), _ = lax.scan(step, init, pt)
    t = jnp.arange(T)
    s_new = jnp.where(t[None, :] <= t[:, None], _mla_scores(q_abs, q_pe, c_new, kpe_new), NEG_INF)
    m_f = jnp.maximum(m, s_new.max(-1))
    w = jnp.exp(s_new - m_f[..., None])
    corr = jnp.exp(m - m_f)
    l = l * corr + w.sum(-1)
    acc = acc * corr[..., None] + jnp.einsum('bhtk,bkr->bhtr', w, c_new, preferred_element_type=jnp.float32)
    return (acc / l[..., None]).swapaxes(1, 2).astype(q_abs.dtype)


def _moba_prompt(q, k, v):
    B, S, H, D = q.shape
    G = H // MOBA_KV_HEADS
    nb = -(-S // MOBA_BLOCK)
    pad = nb * MOBA_BLOCK - S
    def blocks(a):
        a = jnp.pad(a, ((0, 0), (0, pad), (0, 0), (0, 0)))
        return a.reshape(B, nb, MOBA_BLOCK, MOBA_KV_HEADS, D).transpose(0, 3, 1, 2, 4)
    kb, vb = blocks(k), blocks(v)
    n_sel = min(MOBA_TOPK, nb - 1)
    nq = S // MOBA_Q_CHUNK
    q_chunks = q.reshape(B, nq, MOBA_Q_CHUNK, H, D).swapaxes(0, 1)
    if n_sel > 0:
        kmean = jnp.mean(kb.astype(jnp.float32), axis=3)
        gate = jnp.einsum('bskgd,bknd->bkgsn', q.reshape(B, S, MOBA_KV_HEADS, G, D).astype(jnp.float32),
                          kmean).reshape(B, H, S, nb)
        qblk = jnp.arange(S) // MOBA_BLOCK
        gate = jnp.where(jnp.arange(nb)[None, :] < qblk[:, None], gate, NEG_INF)
        _, top_i = lax.top_k(gate, n_sel)
        sel_chunks = top_i.reshape(B, H, nq, MOBA_Q_CHUNK, n_sel).transpose(2, 0, 1, 3, 4)
        xs = (q_chunks, sel_chunks, jnp.arange(nq))
    else:
        xs = (q_chunks, jnp.arange(nq))
    bi = jnp.arange(B)[:, None, None, None]
    hk = (jnp.arange(H) // G)[None, :, None, None]

    def chunk(args):
        if n_sel > 0:
            qc, sel, c = args
        else:
            qc, c = args
        q0 = c * MOBA_Q_CHUNK
        ob = q0 // MOBA_BLOCK
        qpos = q0 + jnp.arange(MOBA_Q_CHUNK)
        own_k = lax.dynamic_index_in_dim(kb, ob, axis=2, keepdims=False)
        own_v = lax.dynamic_index_in_dim(vb, ob, axis=2, keepdims=False)
        kpos = ob * MOBA_BLOCK + jnp.arange(MOBA_BLOCK)
        s_own = jnp.einsum('bqkgd,bknd->bkgqn', qc.reshape(B, MOBA_Q_CHUNK, MOBA_KV_HEADS, G, D), own_k,
                           preferred_element_type=jnp.float32).reshape(B, H, MOBA_Q_CHUNK, MOBA_BLOCK)
        s_own = jnp.where(kpos[None, :] <= qpos[:, None], s_own * MOBA_SCALE, NEG_INF)
        if n_sel > 0:
            ksel = kb[bi, hk, sel]
            vsel = vb[bi, hk, sel]
            s_sel = jnp.einsum('bqhd,bhqnjd->bhqnj', qc, ksel, preferred_element_type=jnp.float32) * MOBA_SCALE
            valid = jnp.arange(n_sel) < ob
            s_sel = jnp.where(valid[:, None], s_sel, NEG_INF).reshape(B, H, MOBA_Q_CHUNK, n_sel * MOBA_BLOCK)
            p = jax.nn.softmax(jnp.concatenate([s_sel, s_own], -1), axis=-1).astype(q.dtype)
            p_sel = p[..., :n_sel * MOBA_BLOCK].reshape(B, H, MOBA_Q_CHUNK, n_sel, MOBA_BLOCK)
            p_own = p[..., n_sel * MOBA_BLOCK:]
            o_sel = jnp.einsum('bhqnj,bhqnjd->bqhd', p_sel, vsel)
        else:
            p_own = jax.nn.softmax(s_own, axis=-1).astype(q.dtype)
        o = jnp.einsum('bkgqn,bknd->bqkgd', p_own.reshape(B, MOBA_KV_HEADS, G, MOBA_Q_CHUNK, MOBA_BLOCK),
                       own_v).reshape(B, MOBA_Q_CHUNK, H, D)
        if n_sel > 0:
            o = o + o_sel
        return o

    o = lax.map(chunk, xs)
    return o.swapaxes(0, 1).reshape(B, S, H, D)


def _moba_sample(q, k_new, v_new, cache_k, cache_v, page_table, layer):
    B, T, H, D = q.shape
    G = H // MOBA_KV_HEADS
    ppb = MOBA_BLOCK // PAGE_SIZE
    n_full = PAST_LEN // MOBA_BLOCK
    tail = PAST_LEN - n_full * MOBA_BLOCK
    n_sel = min(MOBA_TOPK, n_full)
    qg = q.reshape(B, T, MOBA_KV_HEADS, G, D)
    if tail > 0:
        tail_pages = page_table[:, n_full * ppb: PAST_LEN // PAGE_SIZE]
        own_k = jnp.concatenate([cache_k[layer, tail_pages].reshape(B, tail, MOBA_KV_HEADS, D).astype(k_new.dtype), k_new], 1)
        own_v = jnp.concatenate([cache_v[layer, tail_pages].reshape(B, tail, MOBA_KV_HEADS, D).astype(v_new.dtype), v_new], 1)
    else:
        own_k, own_v = k_new, v_new
    L = tail + T
    j = jnp.arange(L)
    t = jnp.arange(T)
    own_mask = (j[None, :] < tail) | (j[None, :] - tail <= t[:, None])
    s_own = jnp.einsum('btkgd,bjkd->bkgtj', qg, own_k, preferred_element_type=jnp.float32).reshape(B, H, T, L)
    s_own = jnp.where(own_mask, s_own * MOBA_SCALE, NEG_INF)
    if n_sel > 0:
        full_pages = page_table[:, :n_full * ppb]
        kmean = jnp.mean(cache_k[layer, full_pages].astype(jnp.float32)
                         .reshape(B, n_full, MOBA_BLOCK, MOBA_KV_HEADS, D), axis=2)
        gate = jnp.einsum('btkgd,bnkd->bkgtn', qg.astype(jnp.float32), kmean).reshape(B, H, T, n_full)
        _, top_i = lax.top_k(gate, n_sel)
        bi = jnp.arange(B)[:, None, None, None, None]
        phys = page_table[bi, top_i[..., None] * ppb + jnp.arange(ppb)]
        hk = (jnp.arange(H) // G)[None, :, None, None, None, None]
        rows = jnp.arange(PAGE_SIZE)
        ksel = cache_k[layer, phys[..., None], rows, hk].reshape(B, H, T, n_sel * MOBA_BLOCK, D).astype(q.dtype)
        vsel = cache_v[layer, phys[..., None], rows, hk].reshape(B, H, T, n_sel * MOBA_BLOCK, D).astype(q.dtype)
        s_sel = jnp.einsum('bthd,bhtjd->bhtj', q, ksel, preferred_element_type=jnp.float32) * MOBA_SCALE
        p = jax.nn.softmax(jnp.concatenate([s_sel, s_own], -1), axis=-1).astype(q.dtype)
        p_sel = p[..., :n_sel * MOBA_BLOCK]
        p_own = p[..., n_sel * MOBA_BLOCK:]
        o_sel = jnp.einsum('bhtj,bhtjd->bthd', p_sel, vsel)
    else:
        p_own = jax.nn.softmax(s_own, axis=-1).astype(q.dtype)
    o = jnp.einsum('bkgtj,bjkd->btkgd', p_own.reshape(B, MOBA_KV_HEADS, G, T, L), own_v).reshape(B, T, H, D)
    if n_sel > 0:
        o = o + o_sel
    return o


def _moe(x2d, lw):
    n_tok = x2d.shape[0]
    n_asg = n_tok * TOP_K
    logits = jnp.dot(x2d, lw['router_w'], preferred_element_type=jnp.float32) + lw['router_b'].astype(jnp.float32)
    top_v, top_e = lax.top_k(logits, TOP_K)
    gates = jax.nn.softmax(top_v, axis=-1)
    flat_e = top_e.reshape(n_asg)
    order = jnp.argsort(flat_e)
    s_e = flat_e[order]
    s_tok = (order // TOP_K).astype(jnp.int32)
    s_gate = gates.reshape(n_asg)[order]
    counts = jnp.bincount(flat_e, length=N_EXPERTS)
    starts = jnp.cumsum(counts) - counts
    padded = (counts + MOE_BLOCK - 1) // MOE_BLOCK * MOE_BLOCK
    pad_ends = jnp.cumsum(padded)
    dest = pad_ends[s_e] - padded[s_e] + jnp.arange(n_asg) - starts[s_e]
    n_blocks = -(-n_asg // MOE_BLOCK) + N_EXPERTS
    n_rows = n_blocks * MOE_BLOCK
    row_tok = jnp.full((n_rows,), n_tok, jnp.int32).at[dest].set(s_tok)
    row_gate = jnp.zeros((n_rows,), jnp.float32).at[dest].set(s_gate)
    block_e = jnp.minimum(jnp.searchsorted(pad_ends, jnp.arange(n_blocks) * MOE_BLOCK, side='right'), N_EXPERTS - 1)
    x_rows = jnp.concatenate([x2d, jnp.zeros((1, D_MODEL), x2d.dtype)], 0)[row_tok]
    x_rows = x_rows.reshape(n_blocks, MOE_BLOCK, D_MODEL)
    w_gu, b_gu, w_dn, b_dn = lw['w_gu'], lw['b_gu'], lw['w_down'], lw['b_down']

    def expert_block(args):
        xb, e = args
        hgu = xb @ w_gu[e] + b_gu[e]
        g = jnp.minimum(hgu[:, :D_FF], SWIGLU_LIMIT)
        u = jnp.clip(hgu[:, D_FF:], -SWIGLU_LIMIT, SWIGLU_LIMIT)
        return ((u + 1.0) * (g * jax.nn.sigmoid(SWIGLU_ALPHA * g))) @ w_dn[e] + b_dn[e]

    y_rows = lax.map(expert_block, (x_rows, block_e)).reshape(n_rows, D_MODEL)
    y = jnp.zeros((n_tok + 1, D_MODEL), y_rows.dtype).at[row_tok].add(y_rows * row_gate[:, None].astype(y_rows.dtype))
    return y[:n_tok]


def _finish_layer(x, conv_y, mla_olat, moba_o, p, lw):
    B, S, _ = x.shape
    mla_v = jnp.einsum('bshr,rhv->bshv', mla_olat, lw['mla_w_uv']).reshape(B, S, MLA_HEADS * MLA_V)
    mixed = jnp.concatenate([conv_y, mla_v, moba_o.reshape(B, S, MOBA_HEADS * MOBA_HEAD_DIM)], -1) @ lw['w_o']
    h = _layer_norm(DEEPNORM_ALPHA * x + mixed, lw['ln1_g'], lw['ln1_b'])
    ffn = _moe(h.reshape(B * S, D_MODEL), lw).reshape(B, S, D_MODEL)
    ple = jax.nn.sigmoid(h @ lw['ple_gate_w'] + lw['ple_gate_b']) * (p @ lw['ple_w'])
    return _layer_norm(DEEPNORM_ALPHA * h + ffn + ple, lw['ln2_g'], lw['ln2_b'])


def setup_inputs(seed: int = 0) -> dict:
    key = jax.random.key(seed)
    ks = iter(jax.random.split(key, 40))

    def nrm(shape, scale=1.0):
        return jax.random.normal(next(ks), shape, jnp.float32) * scale

    n_pages = PAST_LEN // PAGE_SIZE
    n_phys = (DEC_BATCH * n_pages * 5 + 3) // 4
    page_table = jax.random.permutation(next(ks), n_phys)[: DEC_BATCH * n_pages].reshape(DEC_BATCH, n_pages).astype(jnp.int32)
    L = DEPTH
    return {
        'x_prompt': nrm((BATCH, SEQ, D_MODEL)),
        'x_sample': nrm((DEC_BATCH, DEC_SEQ, D_MODEL)),
        'state_conv': nrm((L, DEC_BATCH, CONV_K - 1, CONV_WIDTH)),
        'cache_mla_latent': nrm((L, n_phys, PAGE_SIZE, MLA_KV_LORA)),
        'cache_mla_rope': nrm((L, n_phys, PAGE_SIZE, MLA_ROPE)),
        'cache_moba_k': nrm((L, n_phys, PAGE_SIZE, MOBA_KV_HEADS, MOBA_HEAD_DIM)),
        'cache_moba_v': nrm((L, n_phys, PAGE_SIZE, MOBA_KV_HEADS, MOBA_HEAD_DIM)),
        'page_table': page_table,
        'p_prompt': nrm((L, BATCH, SEQ, PLE_DIM)),
        'p_sample': nrm((L, DEC_BATCH, DEC_SEQ, PLE_DIM)),
        'w_in': nrm((L, D_MODEL, IN_WIDTH), D_MODEL ** -0.5),
        'conv_w': nrm((L, CONV_K, CONV_WIDTH), CONV_K ** -0.5),
        'mla_q_norm': 1.0 + nrm((L, MLA_Q_LORA), 0.02),
        'mla_kv_norm': 1.0 + nrm((L, MLA_KV_LORA), 0.02),
        'mla_w_uq': nrm((L, MLA_Q_LORA, MLA_HEADS * (MLA_NOPE + MLA_ROPE)), MLA_Q_LORA ** -0.5),
        'mla_w_uk': nrm((L, MLA_KV_LORA, MLA_HEADS, MLA_NOPE), MLA_KV_LORA ** -0.5),
        'mla_w_uv': nrm((L, MLA_KV_LORA, MLA_HEADS, MLA_V), MLA_KV_LORA ** -0.5),
        'w_o': nrm((L, MIX_WIDTH, D_MODEL), DEEPNORM_BETA * MIX_WIDTH ** -0.5),
        'ln1_g': 1.0 + nrm((L, D_MODEL), 0.02),
        'ln1_b': nrm((L, D_MODEL), 0.02),
        'router_w': nrm((L, D_MODEL, N_EXPERTS), D_MODEL ** -0.5),
        'router_b': nrm((L, N_EXPERTS), 0.01),
        'w_gu': nrm((L, N_EXPERTS, D_MODEL, 2 * D_FF), D_MODEL ** -0.5),
        'b_gu': nrm((L, N_EXPERTS, 2 * D_FF), 0.01),
        'w_down': nrm((L, N_EXPERTS, D_FF, D_MODEL), DEEPNORM_BETA * D_FF ** -0.5),
        'b_down': nrm((L, N_EXPERTS, D_MODEL), 0.01),
        'ple_w': nrm((L, PLE_DIM, D_MODEL), DEEPNORM_BETA * PLE_DIM ** -0.5),
        'ple_gate_w': nrm((L, D_MODEL, D_MODEL), D_MODEL ** -0.5),
        'ple_gate_b': nrm((L, D_MODEL), 0.01),
        'ln2_g': 1.0 + nrm((L, D_MODEL), 0.02),
        'ln2_b': nrm((L, D_MODEL), 0.02),
    }


def reference(x_prompt, x_sample, state_conv, cache_mla_latent, cache_mla_rope, cache_moba_k, cache_moba_v,
              page_table, p_prompt, p_sample, w_in, conv_w, mla_q_norm, mla_kv_norm, mla_w_uq, mla_w_uk,
              mla_w_uv, w_o, ln1_g, ln1_b, router_w, router_b, w_gu, b_gu, w_down, b_down, ple_w,
              ple_gate_w, ple_gate_b, ln2_g, ln2_b):
    B, S, _ = x_prompt.shape
    T = x_sample.shape[1]
    pos_p = jnp.arange(S, dtype=jnp.int32)
    pos_s = PAST_LEN + jnp.arange(T, dtype=jnp.int32)
    conv0 = jnp.zeros((B, CONV_K - 1, CONV_WIDTH), x_prompt.dtype)
    yp, ys = x_prompt, x_sample
    conv_p, conv_s, lat_p, lat_s, rope_p, rope_s = [], [], [], [], [], []
    kp_l, ks_l, vp_l, vs_l = [], [], [], []
    for i in range(DEPTH):
        lw = {'w_in': w_in[i], 'conv_w': conv_w[i], 'mla_q_norm': mla_q_norm[i], 'mla_kv_norm': mla_kv_norm[i],
              'mla_w_uq': mla_w_uq[i], 'mla_w_uk': mla_w_uk[i], 'mla_w_uv': mla_w_uv[i], 'w_o': w_o[i],
              'ln1_g': ln1_g[i], 'ln1_b': ln1_b[i], 'router_w': router_w[i], 'router_b': router_b[i],
              'w_gu': w_gu[i], 'b_gu': b_gu[i], 'w_down': w_down[i], 'b_down': b_down[i], 'ple_w': ple_w[i],
              'ple_gate_w': ple_gate_w[i], 'ple_gate_b': ple_gate_b[i], 'ln2_g': ln2_g[i], 'ln2_b': ln2_b[i]}
        (ch, cb, cc), (qa, qpe, ckv, kpe), (mq, mk, mv) = _mixer_projections(yp, pos_p, lw)
        conv_y, cst = _conv_mixer(ch, cb, cc, conv0, lw['conv_w'])
        mla_o = _mla_prompt(qa, qpe, ckv, kpe)
        moba_o = _moba_prompt(mq, mk, mv)
        yp = _finish_layer(yp, conv_y, mla_o, moba_o, p_prompt[i], lw)
        conv_p.append(cst)
        lat_p.append(ckv)
        rope_p.append(kpe)
        kp_l.append(mk)
        vp_l.append(mv)
        (ch, cb, cc), (qa, qpe, ckv, kpe), (mq, mk, mv) = _mixer_projections(ys, pos_s, lw)
        conv_y, cst = _conv_mixer(ch, cb, cc, state_conv[i], lw['conv_w'])
        mla_o = _mla_sample(qa, qpe, ckv, kpe, cache_mla_latent, cache_mla_rope, page_table, i)
        moba_o = _moba_sample(mq, mk, mv, cache_moba_k, cache_moba_v, page_table, i)
        ys = _finish_layer(ys, conv_y, mla_o, moba_o, p_sample[i], lw)
        conv_s.append(cst)
        lat_s.append(ckv)
        rope_s.append(kpe)
        ks_l.append(mk)
        vs_l.append(mv)
    conv_prompt, conv_sample = jnp.stack(conv_p), jnp.stack(conv_s)
    mla_latent_prompt, mla_latent_sample = jnp.stack(lat_p), jnp.stack(lat_s)
    mla_rope_prompt, mla_rope_sample = jnp.stack(rope_p), jnp.stack(rope_s)
    moba_k_prompt, moba_k_sample = jnp.stack(kp_l), jnp.stack(ks_l)
    moba_v_prompt, moba_v_sample = jnp.stack(vp_l), jnp.stack(vs_l)
    return (yp, ys, conv_prompt, conv_sample, mla_latent_prompt, mla_latent_sample,
            mla_rope_prompt, mla_rope_sample, moba_k_prompt, moba_k_sample, moba_v_prompt, moba_v_sample)
```

```python
import functools

import numpy as np
import jax
import jax.numpy as jnp
from jax import lax
from jax.experimental import pallas as pl
from jax.experimental.pallas import tpu as pltpu

F32 = jnp.float32
BF16 = jnp.bfloat16
I32 = jnp.int32

LANES = 128
PAGE = 128
CONV_W = 256
CONV_K = 3
MLA_H = 6
MLA_QL = 256
MLA_R = 128
MLA_NOPE = 64
MLA_ROPE = 32
MLA_V = 64
MLA_THETA = 10000.0
MLA_KW = MLA_R + MLA_ROPE
MOBA_H = 6
MOBA_KVH = 2
MOBA_G = MOBA_H // MOBA_KVH
MOBA_D = 64
MOBA_BLK = 256
MOBA_TOPK = 3
MOBA_ROT = MOBA_D // 4
MOBA_THETA = 500000.0
N_EXP = 32
TOP_K = 4
D_FF = 1024
SWIGLU_LIMIT = 7.0
SWIGLU_ALPHA = 1.702
LN_EPS = 1e-5
RMS_EPS = 1e-6
NEG = -1e30
MLA_SCALE = (MLA_NOPE + MLA_ROPE) ** -0.5
MOBA_SCALE = MOBA_D ** -0.5

OFF_H, OFF_B, OFF_C, OFF_QL, OFF_KV, OFF_MQ, OFF_MK, OFF_MV, OFF_KPE = 0, 256, 512, 768, 1024, 1152, 1536, 1664, 1792
IN_W = 1920
UQ_W = 640

TOK_TILE = 512
ATT_TILE = 256
MOE_TILE = 256
PAGES_PER_CHUNK = 16
VMEM_LIMIT = 56 * 1024 * 1024


def _nt(a, b):
    return lax.dot_general(a, b, (((1,), (1,)), ((), ())), preferred_element_type=F32)


def _dot(a, b):
    return jnp.dot(a, b, preferred_element_type=F32)


def _params(sem):
    return pltpu.CompilerParams(dimension_semantics=sem, vmem_limit_bytes=VMEM_LIMIT)


def _rope_lane_tables(pos, dim, theta, period):
    half = dim // 2
    inv = theta ** (-jnp.arange(0, dim, 2, dtype=F32) / dim)
    ang = pos.astype(F32)[:, None] * inv[None, :]
    cos, sin = jnp.cos(ang), jnp.sin(ang)
    d = np.arange(LANES) % period
    j = d % half
    rot = jnp.asarray(d < dim)[None, :]
    first = jnp.asarray(d < half)[None, :]
    c = jnp.where(rot, cos[:, j], 1.0)
    s1 = jnp.where(rot & first, -sin[:, j], 0.0)
    s2 = jnp.where(rot & ~first, sin[:, j], 0.0)
    return c, s1, s2


def _rope_table(seq, past, t_new, tile):
    pos = jnp.concatenate([jnp.arange(seq, dtype=I32), past + (jnp.arange(tile, dtype=I32) % t_new)])
    mla = _rope_lane_tables(pos, MLA_ROPE, MLA_THETA, MLA_ROPE)
    moba = _rope_lane_tables(pos, MOBA_ROT, MOBA_THETA, MOBA_D)
    return jnp.concatenate(list(mla) + list(moba), axis=1)


def _rope(x, tab, base, half):
    c = tab[:, base:base + LANES]
    s1 = tab[:, base + LANES:base + 2 * LANES]
    s2 = tab[:, base + 2 * LANES:base + 3 * LANES]
    return x * c + pltpu.roll(x, LANES - half, 1) * s1 + pltpu.roll(x, half, 1) * s2


def _proj_kernel(x_ref, tab_ref, win_ref, wuq_ref, wuk_ref, qn_ref, kvn_ref,
                 u_ref, cvb_ref, qcat_ref, kcat_ref, ckv_ref, kpe_ref, mq_ref, mk_ref, mv_ref, mkv_ref):
    y = _dot(x_ref[...].astype(BF16), win_ref[...])
    tab = tab_ref[...]
    u_ref[...] = y[:, OFF_C:OFF_C + CONV_W] * y[:, OFF_H:OFF_H + CONV_W]
    cvb_ref[...] = y[:, OFF_B:OFF_B + CONV_W]
    q_lat = y[:, OFF_QL:OFF_QL + MLA_QL]
    q_n = q_lat * lax.rsqrt(jnp.mean(q_lat * q_lat, axis=1, keepdims=True) + RMS_EPS) * qn_ref[...]
    q = _dot(q_n.astype(BF16), wuq_ref[...])
    n_nope = MLA_H * MLA_NOPE
    q_abs = _dot(q[:, :n_nope].astype(BF16), wuk_ref[...]) * MLA_SCALE
    q_pe = [_rope(q[:, n_nope + k * LANES:n_nope + (k + 1) * LANES], tab, 0, MLA_ROPE // 2) * MLA_SCALE
            for k in range(2)]
    per_slab = LANES // MLA_ROPE
    for h in range(MLA_H):
        qcat_ref[h, :, 0:MLA_R] = q_abs[:, h * MLA_R:(h + 1) * MLA_R].astype(BF16)
        lo = (h % per_slab) * MLA_ROPE
        qcat_ref[h, :, MLA_R:MLA_KW] = q_pe[h // per_slab][:, lo:lo + MLA_ROPE].astype(BF16)
    kv_lat = y[:, OFF_KV:OFF_KV + MLA_R]
    c_kv = kv_lat * lax.rsqrt(jnp.mean(kv_lat * kv_lat, axis=1, keepdims=True) + RMS_EPS) * kvn_ref[...]
    k_pe = _rope(y[:, OFF_KPE:OFF_KPE + LANES], tab, 0, MLA_ROPE // 2)[:, :MLA_ROPE]
    ckv_ref[...] = c_kv
    kpe_ref[...] = k_pe
    kcat_ref[:, 0:MLA_R] = c_kv.astype(BF16)
    kcat_ref[:, MLA_R:MLA_KW] = k_pe.astype(BF16)
    for k in range(MOBA_H * MOBA_D // LANES):
        slab = _rope(y[:, OFF_MQ + k * LANES:OFF_MQ + (k + 1) * LANES], tab, 3 * LANES, MOBA_ROT // 2)
        mq_ref[:, k * LANES:(k + 1) * LANES] = (slab * MOBA_SCALE).astype(BF16)
    mk = _rope(y[:, OFF_MK:OFF_MK + LANES], tab, 3 * LANES, MOBA_ROT // 2)
    mv = y[:, OFF_MV:OFF_MV + LANES]
    mk_ref[...] = mk
    mv_ref[...] = mv
    mkv_ref[:, 0:LANES] = mk.astype(BF16)
    mkv_ref[:, LANES:2 * LANES] = mv.astype(BF16)


def _proj(x, tab, win, wuq, wuk, qn, kvn, *, n_prompt, seq, tm):
    n = x.shape[0]
    n_pt = n_prompt // tm
    per_seq = seq // tm

    def tab_map(i):
        return (jnp.where(i < n_pt, i % per_seq, per_seq), 0)

    row = lambda w: pl.BlockSpec((tm, w), lambda i: (i, 0))
    full = lambda a: pl.BlockSpec(a.shape, lambda i: (0,) * a.ndim)
    out_shape = [
        jax.ShapeDtypeStruct((n, CONV_W), F32), jax.ShapeDtypeStruct((n, CONV_W), F32),
        jax.ShapeDtypeStruct((MLA_H, n, MLA_KW), BF16), jax.ShapeDtypeStruct((n, MLA_KW), BF16),
        jax.ShapeDtypeStruct((n, MLA_R), F32), jax.ShapeDtypeStruct((n, MLA_ROPE), F32),
        jax.ShapeDtypeStruct((n, MOBA_H * MOBA_D), BF16), jax.ShapeDtypeStruct((n, LANES), F32),
        jax.ShapeDtypeStruct((n, LANES), F32), jax.ShapeDtypeStruct((n, 2 * LANES), BF16),
    ]
    out_specs = [
        row(CONV_W), row(CONV_W), pl.BlockSpec((MLA_H, tm, MLA_KW), lambda i: (0, i, 0)), row(MLA_KW),
        row(MLA_R), row(MLA_ROPE), row(MOBA_H * MOBA_D), row(LANES), row(LANES), row(2 * LANES),
    ]
    return pl.pallas_call(
        _proj_kernel, grid=(n // tm,),
        in_specs=[row(x.shape[1]), pl.BlockSpec((tm, tab.shape[1]), tab_map),
                  full(win), full(wuq), full(wuk), full(qn), full(kvn)],
        out_specs=out_specs, out_shape=out_shape,
        compiler_params=_params(("arbitrary",)), name="proj",
    )(x, tab, win, wuq, wuk, qn, kvn)


def _softmax_update(carry, s, v):
    m, l, acc = carry
    m_new = jnp.maximum(m, jnp.max(s, axis=1, keepdims=True))
    p = jnp.exp(s - m_new)
    corr = jnp.exp(m - m_new)
    l = l * corr + jnp.sum(p, axis=1, keepdims=True)
    acc = acc * corr + _dot(p.astype(BF16), v)
    return m_new, l, acc


def _softmax_init(rows, width):
    return (jnp.full((rows, 1), NEG, F32), jnp.zeros((rows, 1), F32), jnp.zeros((rows, width), F32))


def _causal_mask(rows, t_q, cols):
    r = lax.broadcasted_iota(I32, (rows, cols), 0) % t_q
    c = lax.broadcasted_iota(I32, (rows, cols), 1)
    return c <= r


def _mla_prompt_kernel(q_ref, k_ref, wuv_ref, o_ref, *, tq):
    i = pl.program_id(1)
    rows = MLA_H * tq
    q = q_ref[...].reshape(rows, MLA_KW)

    def tile(j):
        k = k_ref[pl.ds(pl.multiple_of(j * tq, tq), tq), :]
        return _nt(q, k), k[:, :MLA_R]

    s, v = tile(i)
    s = jnp.where(_causal_mask(rows, tq, tq), s, NEG)
    carry = _softmax_update(_softmax_init(rows, MLA_R), s, v)
    carry = lax.fori_loop(0, i, lambda j, c: _softmax_update(c, *tile(j)), carry)
    _, l, acc = carry
    o = acc / l
    outs = [_dot(o[h * tq:(h + 1) * tq].astype(BF16), wuv_ref[h]) for h in range(MLA_H)]
    o_ref[...] = jnp.concatenate(outs, axis=1).astype(o_ref.dtype)


def _mla_prompt(qcat, kcat, wuv, *, batch, seq, tq):
    nq = seq // tq
    return pl.pallas_call(
        functools.partial(_mla_prompt_kernel, tq=tq), grid=(batch, nq),
        in_specs=[pl.BlockSpec((MLA_H, tq, MLA_KW), lambda b, i: (0, b * nq + i, 0)),
                  pl.BlockSpec((seq, MLA_KW), lambda b, i: (b, 0)),
                  pl.BlockSpec(wuv.shape, lambda b, i: (0, 0, 0))],
        out_specs=pl.BlockSpec((tq, MLA_H * MLA_V), lambda b, i: (b * nq + i, 0)),
        out_shape=jax.ShapeDtypeStruct((batch * seq, MLA_H * MLA_V), BF16),
        compiler_params=_params(("arbitrary", "arbitrary")), name="mla_prompt",
    )(qcat, kcat, wuv)


def _moba_prompt_kernel(q_ref, kv_ref, kf_ref, o_ref, km_ref, *, nb):
    i = pl.program_id(1)
    tq = MOBA_BLK
    rows = MOBA_G * tq

    @pl.when(i == 0)
    def _():
        km_ref[...] = jnp.zeros(km_ref.shape, F32)
        for n in range(nb):
            km_ref[n:n + 1, :] = jnp.mean(kf_ref[n * tq:(n + 1) * tq, :], axis=0, keepdims=True)

    q_all = q_ref[...]
    lane = lax.broadcasted_iota(I32, (rows, LANES), 1)
    blk_row = lax.broadcasted_iota(I32, (LANES, tq), 0)
    causal = _causal_mask(rows, tq, tq)
    pieces = []
    for g in range(MOBA_KVH):
        qg = jnp.concatenate([q_all[:, (MOBA_G * g + hh) * MOBA_D:(MOBA_G * g + hh + 1) * MOBA_D]
                              for hh in range(MOBA_G)], axis=0)
        gate = _nt(qg, km_ref[:, g * MOBA_D:(g + 1) * MOBA_D].astype(BF16))
        gate = jnp.where(lane < i, gate, NEG)
        cnt = jnp.zeros((rows, LANES), I32)
        for m in range(nb - 1):
            col = gate[:, m:m + 1]
            beats = (col > gate) | ((col == gate) & (lane > m))
            cnt = cnt + beats.astype(I32)
        sel = jnp.where((lane < i) & (cnt < MOBA_TOPK), 1.0, 0.0).astype(BF16)

        def kv_tile(n):
            t = kv_ref[pl.ds(pl.multiple_of(n * tq, tq), tq), :]
            return t[:, g * MOBA_D:(g + 1) * MOBA_D], t[:, LANES + g * MOBA_D:LANES + (g + 1) * MOBA_D]

        k, v = kv_tile(i)
        s = jnp.where(causal, _nt(qg, k), NEG)
        carry = _softmax_update(_softmax_init(rows, MOBA_D), s, v)

        def past(n, c):
            k, v = kv_tile(n)
            picked = _dot(sel, (blk_row == n).astype(BF16))
            return _softmax_update(c, jnp.where(picked > 0.5, _nt(qg, k), NEG), v)

        _, l, acc = lax.fori_loop(0, i, past, carry)
        o = acc / l
        pieces += [o[hh * tq:(hh + 1) * tq] for hh in range(MOBA_G)]
    o_ref[...] = jnp.concatenate(pieces, axis=1).astype(o_ref.dtype)


def _moba_prompt(mq, mkv, mk, *, batch, seq):
    nb = seq // MOBA_BLK
    return pl.pallas_call(
        functools.partial(_moba_prompt_kernel, nb=nb), grid=(batch, nb),
        in_specs=[pl.BlockSpec((MOBA_BLK, MOBA_H * MOBA_D), lambda b, i: (b * nb + i, 0)),
                  pl.BlockSpec((seq, 2 * LANES), lambda b, i: (b, 0)),
                  pl.BlockSpec((seq, LANES), lambda b, i: (b, 0))],
        out_specs=pl.BlockSpec((MOBA_BLK, MOBA_H * MOBA_D), lambda b, i: (b * nb + i, 0)),
        out_shape=jax.ShapeDtypeStruct((batch * seq, MOBA_H * MOBA_D), BF16),
        scratch_shapes=[pltpu.VMEM((LANES, LANES), F32)],
        compiler_params=_params(("arbitrary", "arbitrary")), name="moba_prompt",
    )(mq, mkv, mk)


def _page_copies(pt_ref, b, chunk, slot, srcs, bufs, sems, layer, ppc):
    out = []
    for j in range(ppc):
        pg = pt_ref[b, chunk * ppc + j]
        for a, (src, buf, sem) in enumerate(zip(srcs, bufs, sems)):
            out.append(pltpu.make_async_copy(src.at[layer, pg], buf.at[slot, j], sem.at[slot, j]))
    return out


def _mla_sample_kernel(pt_ref, q_ref, knew_ref, wuv_ref, lat_hbm, rope_hbm, o_ref,
                       lat_buf, rope_buf, lat_sem, rope_sem, m_ref, l_ref, acc_ref,
                       *, layer, ppc, t_new):
    b, c = pl.program_id(0), pl.program_id(1)
    nbatch, nchunk = pl.num_programs(0), pl.num_programs(1)
    step = b * nchunk + c
    slot = step % 2
    rows = MLA_H * t_new
    copies = functools.partial(_page_copies, pt_ref, srcs=(lat_hbm, rope_hbm), bufs=(lat_buf, rope_buf),
                               sems=(lat_sem, rope_sem), layer=layer, ppc=ppc)

    @pl.when(step == 0)
    def _():
        for cp in copies(b, c, slot):
            cp.start()

    @pl.when(step + 1 < nbatch * nchunk)
    def _():
        last = c + 1 == nchunk
        for cp in copies(jnp.where(last, b + 1, b), jnp.where(last, 0, c + 1), 1 - slot):
            cp.start()

    for cp in copies(b, c, slot):
        cp.wait()

    @pl.when(c == 0)
    def _():
        m_ref[...] = jnp.full(m_ref.shape, NEG, F32)
        l_ref[...] = jnp.zeros(l_ref.shape, F32)
        acc_ref[...] = jnp.zeros(acc_ref.shape, F32)

    q = q_ref[0]
    lat = lat_buf[slot].reshape(ppc * PAGE, MLA_R).astype(BF16)
    rope = rope_buf[slot].reshape(ppc * PAGE, MLA_ROPE).astype(BF16)
    s = _nt(q[:, :MLA_R], lat) + _nt(q[:, MLA_R:], rope)
    m, l, acc = _softmax_update((m_ref[...], l_ref[...], acc_ref[...]), s, lat)
    m_ref[...], l_ref[...], acc_ref[...] = m, l, acc

    @pl.when(c == nchunk - 1)
    def _():
        k_new = knew_ref[0].astype(BF16)
        s_new = jnp.where(_causal_mask(rows, t_new, t_new), _nt(q, k_new), NEG)
        _, l2, acc2 = _softmax_update((m, l, acc), s_new, k_new[:, :MLA_R])
        o = acc2 / l2
        outs = [_dot(o[h * t_new:(h + 1) * t_new].astype(BF16), wuv_ref[h]) for h in range(MLA_H)]
        o_ref[0] = jnp.concatenate(outs, axis=1)


def _mla_sample(page_table, q, k_new, wuv, cache_lat, cache_rope, *, layer):
    nbatch, n_pages = page_table.shape
    t_new = k_new.shape[1]
    ppc = min(PAGES_PER_CHUNK, n_pages)
    assert n_pages % ppc == 0
    rows = MLA_H * t_new
    grid_spec = pltpu.PrefetchScalarGridSpec(
        num_scalar_prefetch=1, grid=(nbatch, n_pages // ppc),
        in_specs=[pl.BlockSpec((1, rows, MLA_KW), lambda b, c, pt: (b, 0, 0)),
                  pl.BlockSpec((1, t_new, MLA_KW), lambda b, c, pt: (b, 0, 0)),
                  pl.BlockSpec(wuv.shape, lambda b, c, pt: (0, 0, 0)),
                  pl.BlockSpec(memory_space=pl.ANY), pl.BlockSpec(memory_space=pl.ANY)],
        out_specs=pl.BlockSpec((1, t_new, MLA_H * MLA_V), lambda b, c, pt: (b, 0, 0)),
        scratch_shapes=[pltpu.VMEM((2, ppc, PAGE, MLA_R), F32), pltpu.VMEM((2, ppc, PAGE, MLA_ROPE), F32),
                        pltpu.SemaphoreType.DMA((2, ppc)), pltpu.SemaphoreType.DMA((2, ppc)),
                        pltpu.VMEM((rows, 1), F32), pltpu.VMEM((rows, 1), F32), pltpu.VMEM((rows, MLA_R), F32)])
    return pl.pallas_call(
        functools.partial(_mla_sample_kernel, layer=layer, ppc=ppc, t_new=t_new), grid_spec=grid_spec,
        out_shape=jax.ShapeDtypeStruct((nbatch, t_new, MLA_H * MLA_V), F32),
        compiler_params=_params(("arbitrary", "arbitrary")), name="mla_sample",
    )(page_table, q, k_new, wuv, cache_lat, cache_rope)


def _moba_sample_kernel(pt_ref, q_ref, knew_ref, vnew_ref, k_hbm, v_hbm, o_ref,
                        buf, sem, s_ref, km_ref, sel_ref, m_ref, l_ref, acc_ref,
                        *, layer, ppc, t_new, n_full):
    b, ph, c = pl.program_id(0), pl.program_id(1), pl.program_id(2)
    nbatch, nchunk = pl.num_programs(0), pl.num_programs(2)
    step = (b * 2 + ph) * nchunk + c
    slot = step % 2
    rows = q_ref.shape[1]
    keys = ppc * PAGE
    bpc = keys // MOBA_BLK
    copies_k = functools.partial(_page_copies, pt_ref, srcs=(k_hbm,), bufs=(buf,), sems=(sem,), layer=layer, ppc=ppc)
    copies_v = functools.partial(_page_copies, pt_ref, srcs=(v_hbm,), bufs=(buf,), sems=(sem,), layer=layer, ppc=ppc)

    def start(bb, pp, cc, sl):
        @pl.when(pp == 0)
        def _():
            for cp in copies_k(bb, cc, sl):
                cp.start()

        @pl.when(pp == 1)
        def _():
            for cp in copies_v(bb, cc, sl):
                cp.start()

    @pl.when(step == 0)
    def _():
        start(b, ph, c, slot)

    @pl.when(step + 1 < nbatch * 2 * nchunk)
    def _():
        last_c = c + 1 == nchunk
        wrap = last_c & (ph == 1)
        start(jnp.where(wrap, b + 1, b), jnp.where(last_c, 1 - ph, ph), jnp.where(last_c, 0, c + 1), 1 - slot)

    @pl.when(ph == 0)
    def _():
        for cp in copies_k(b, c, slot):
            cp.wait()

    @pl.when(ph == 1)
    def _():
        for cp in copies_v(b, c, slot):
            cp.wait()

    q = q_ref[0]
    lane = lax.broadcasted_iota(I32, (rows, LANES), 1)
    new_mask = _causal_mask(rows, t_new, t_new)

    def new_scores():
        return jnp.where(new_mask, _nt(q, knew_ref[0].astype(BF16)), NEG)

    def picked(cc):
        blk = lax.broadcasted_iota(I32, (LANES, keys), 0)
        key_blk = cc * bpc + lax.broadcasted_iota(I32, (LANES, keys), 1) // MOBA_BLK
        return _dot(sel_ref[...], (blk == key_blk).astype(BF16)) > 0.5

    @pl.when(ph == 0)
    def _():
        @pl.when(c == 0)
        def _():
            km_ref[...] = jnp.zeros(km_ref.shape, F32)

        kf = buf[slot].reshape(keys, LANES)
        s_ref[c] = _nt(q, kf.astype(BF16))
        km_ref[pl.ds(pl.multiple_of(c * bpc, bpc), bpc), :] = jnp.mean(kf.reshape(bpc, MOBA_BLK, LANES), axis=1)

        @pl.when(c == nchunk - 1)
        def _():
            gate = jnp.where(lane < n_full, _nt(q, km_ref[...].astype(BF16)), NEG)
            sel = jnp.zeros((rows, LANES), F32)
            for _ in range(MOBA_TOPK):
                best = jnp.max(gate, axis=1, keepdims=True)
                first = jnp.min(jnp.where(gate == best, lane, LANES), axis=1, keepdims=True)
                hit = lane == first
                sel = jnp.where(hit, 1.0, sel)
                gate = jnp.where(hit, NEG * 2, gate)
            sel_ref[...] = sel.astype(BF16)
            m = jnp.max(new_scores(), axis=1, keepdims=True)
            for cc in range(s_ref.shape[0]):
                m = jnp.maximum(m, jnp.max(jnp.where(picked(cc), s_ref[cc], NEG), axis=1, keepdims=True))
            m_ref[...] = m
            l_ref[...] = jnp.zeros(l_ref.shape, F32)
            acc_ref[...] = jnp.zeros(acc_ref.shape, F32)

    @pl.when(ph == 1)
    def _():
        vb = buf[slot].reshape(keys, LANES).astype(BF16)
        p = jnp.where(picked(c), jnp.exp(s_ref[c] - m_ref[...]), 0.0)
        l = l_ref[...] + jnp.sum(p, axis=1, keepdims=True)
        acc = acc_ref[...] + _dot(p.astype(BF16), vb)
        l_ref[...], acc_ref[...] = l, acc

        @pl.when(c == nchunk - 1)
        def _():
            p_new = jnp.where(new_mask, jnp.exp(new_scores() - m_ref[...]), 0.0)
            l2 = l + jnp.sum(p_new, axis=1, keepdims=True)
            acc2 = acc + _dot(p_new.astype(BF16), vnew_ref[0].astype(BF16))
            o_ref[0] = acc2 / l2


def _moba_sample(page_table, q_bd, k_new, v_new, cache_k, cache_v, *, layer):
    nbatch, n_pages = page_table.shape
    t_new = k_new.shape[1]
    rows = q_bd.shape[1]
    ppc = min(PAGES_PER_CHUNK, n_pages)
    keys = ppc * PAGE
    past = n_pages * PAGE
    assert n_pages % ppc == 0 and keys % MOBA_BLK == 0 and past % MOBA_BLK == 0
    n_full = past // MOBA_BLK
    assert MOBA_TOPK <= n_full <= LANES
    nchunk = n_pages // ppc
    grid_spec = pltpu.PrefetchScalarGridSpec(
        num_scalar_prefetch=1, grid=(nbatch, 2, nchunk),
        in_specs=[pl.BlockSpec((1, rows, LANES), lambda b, p, c, pt: (b, 0, 0)),
                  pl.BlockSpec((1, t_new, LANES), lambda b, p, c, pt: (b, 0, 0)),
                  pl.BlockSpec((1, t_new, LANES), lambda b, p, c, pt: (b, 0, 0)),
                  pl.BlockSpec(memory_space=pl.ANY), pl.BlockSpec(memory_space=pl.ANY)],
        out_specs=pl.BlockSpec((1, rows, LANES), lambda b, p, c, pt: (b, 0, 0)),
        scratch_shapes=[pltpu.VMEM((2, ppc, PAGE, LANES), F32), pltpu.SemaphoreType.DMA((2, ppc)),
                        pltpu.VMEM((nchunk, rows, keys), F32), pltpu.VMEM((LANES, LANES), F32),
                        pltpu.VMEM((rows, LANES), BF16),
                        pltpu.VMEM((rows, 1), F32), pltpu.VMEM((rows, 1), F32), pltpu.VMEM((rows, LANES), F32)])
    return pl.pallas_call(
        functools.partial(_moba_sample_kernel, layer=layer, ppc=ppc, t_new=t_new, n_full=n_full),
        grid_spec=grid_spec, out_shape=jax.ShapeDtypeStruct((nbatch, rows, LANES), F32),
        compiler_params=_params(("arbitrary", "arbitrary", "arbitrary")), name="moba_sample",
    )(page_table, q_bd, k_new, v_new, cache_k, cache_v)


def _layer_norm(x, g, b):
    mu = jnp.mean(x, axis=1, keepdims=True)
    xc = x - mu
    var = jnp.mean(xc * xc, axis=1, keepdims=True)
    return xc * lax.rsqrt(var + LN_EPS) * g + b


def _post1_kernel(x_ref, u_ref, u1_ref, u2_ref, cvb_ref, cw_ref, mla_ref, moba_ref, p_ref,
                  wo_ref, g_ref, b_ref, rwh_ref, rwl_ref, rb_ref, gw_ref, gb_ref, pw_ref,
                  h_ref, ple_ref, topi_ref, topg_ref, *, alpha):
    cw = cw_ref[...]
    conv = cvb_ref[...] * (cw[0:1] * u2_ref[...] + cw[1:2] * u1_ref[...] + cw[2:3] * u_ref[...])
    mix_in = jnp.concatenate([conv.astype(BF16), mla_ref[...], moba_ref[...]], axis=1)
    h = _layer_norm(alpha * x_ref[...] + _dot(mix_in, wo_ref[...]), g_ref[...], b_ref[...])
    h_ref[...] = h
    h_hi = h.astype(BF16)
    h_lo = (h - h_hi.astype(F32)).astype(BF16)
    ple_ref[...] = jax.nn.sigmoid(_dot(h_hi, gw_ref[...]) + gb_ref[...]) * _dot(p_ref[...].astype(BF16), pw_ref[...])
    logits = _dot(h_hi, rwh_ref[...]) + (_dot(h_lo, rwh_ref[...]) + _dot(h_hi, rwl_ref[...])) + rb_ref[...]
    lane = lax.broadcasted_iota(I32, logits.shape, 1)
    logits = jnp.where(lane < N_EXP, logits, NEG)
    topi = jnp.zeros(logits.shape, I32)
    topv = jnp.zeros(logits.shape, F32)
    v0 = None
    den = jnp.zeros((logits.shape[0], 1), F32)
    for k in range(TOP_K):
        best = jnp.max(logits, axis=1, keepdims=True)
        first = jnp.min(jnp.where(logits == best, lane, LANES), axis=1, keepdims=True)
        v0 = best if v0 is None else v0
        e = jnp.exp(best - v0)
        den = den + e
        topi = jnp.where(lane == k, first, topi)
        topv = jnp.where(lane == k, e, topv)
        logits = jnp.where(lane == first, NEG * 2, logits)
    topi_ref[...] = topi
    topg_ref[...] = topv / den


def _post1(x, u, u1, u2, cvb, cw, mla_v, moba_o, p, wo, g, b, rwh, rwl, rb, gw, gb, pw, *, alpha, tm):
    n, d = x.shape
    row = lambda a: pl.BlockSpec((tm, a.shape[1]), lambda i: (i, 0))
    full = lambda a: pl.BlockSpec(a.shape, lambda i: (0,) * a.ndim)
    ins = (x, u, u1, u2, cvb, cw, mla_v, moba_o, p, wo, g, b, rwh, rwl, rb, gw, gb, pw)
    is_row = (1, 1, 1, 1, 1, 0, 1, 1, 1, 0, 0, 0, 0, 0, 0, 0, 0, 0)
    out_shape = [jax.ShapeDtypeStruct((n, d), F32), jax.ShapeDtypeStruct((n, d), F32),
                 jax.ShapeDtypeStruct((n, LANES), I32), jax.ShapeDtypeStruct((n, LANES), F32)]
    return pl.pallas_call(
        functools.partial(_post1_kernel, alpha=alpha), grid=(n // tm,),
        in_specs=[row(a) if r else full(a) for a, r in zip(ins, is_row)],
        out_specs=[pl.BlockSpec((tm, s.shape[1]), lambda i: (i, 0)) for s in out_shape],
        out_shape=out_shape, compiler_params=_params(("arbitrary",)), name="post1",
    )(*ins)


def _moe_kernel(be_ref, nu_ref, tok_ref, tokn_ref, h_hbm, wgu_ref, bgu_ref, wdn_ref, bdn_ref, y_ref,
                xbuf, sem, *, blk):
    i = pl.program_id(0)
    n_used = nu_ref[0]
    slot = i % 2

    def gather(tok, sl, fn):
        def body(r, _):
            fn(pltpu.make_async_copy(h_hbm.at[pl.ds(tok[0, 0, r], 1)], xbuf.at[sl, pl.ds(r, 1)], sem.at[sl]))
            return 0
        lax.fori_loop(0, blk, body, 0)

    @pl.when(i == 0)
    def _():
        gather(tok_ref, slot, lambda cp: cp.start())

    @pl.when(i + 1 < n_used)
    def _():
        gather(tokn_ref, 1 - slot, lambda cp: cp.start())

    @pl.when(i < n_used)
    def _():
        gather(tok_ref, slot, lambda cp: cp.wait())
        x = xbuf[slot].astype(BF16)
        hgu = _dot(x, wgu_ref[0]) + bgu_ref[0]
        g = jnp.minimum(hgu[:, :D_FF], SWIGLU_LIMIT)
        u = jnp.clip(hgu[:, D_FF:], -SWIGLU_LIMIT, SWIGLU_LIMIT)
        act = (u + 1.0) * (g * jax.nn.sigmoid(SWIGLU_ALPHA * g))
        y_ref[...] = _dot(act.astype(BF16), wdn_ref[0]) + bdn_ref[0]

    @pl.when(i >= n_used)
    def _():
        y_ref[...] = jnp.zeros(y_ref.shape, F32)


def _moe(block_e, n_used, row_tok, h, wgu, bgu, wdn, bdn, *, blk):
    n_blocks = row_tok.shape[0]
    d = h.shape[1]
    grid_spec = pltpu.PrefetchScalarGridSpec(
        num_scalar_prefetch=2, grid=(n_blocks,),
        in_specs=[pl.BlockSpec((1, 1, blk), lambda i, be, nu: (i, 0, 0), memory_space=pltpu.SMEM),
                  pl.BlockSpec((1, 1, blk), lambda i, be, nu: (jnp.minimum(i + 1, n_blocks - 1), 0, 0),
                               memory_space=pltpu.SMEM),
                  pl.BlockSpec(memory_space=pl.ANY),
                  pl.BlockSpec((1,) + wgu.shape[1:], lambda i, be, nu: (be[i], 0, 0)),
                  pl.BlockSpec((1,) + bgu.shape[1:], lambda i, be, nu: (be[i], 0, 0)),
                  pl.BlockSpec((1,) + wdn.shape[1:], lambda i, be, nu: (be[i], 0, 0)),
                  pl.BlockSpec((1,) + bdn.shape[1:], lambda i, be, nu: (be[i], 0, 0))],
        out_specs=pl.BlockSpec((blk, d), lambda i, be, nu: (i, 0)),
        scratch_shapes=[pltpu.VMEM((2, blk, d), F32), pltpu.SemaphoreType.DMA((2,))])
    return pl.pallas_call(
        functools.partial(_moe_kernel, blk=blk), grid_spec=grid_spec,
        out_shape=jax.ShapeDtypeStruct((n_blocks * blk, d), F32),
        compiler_params=_params(("arbitrary",)), name="moe",
    )(block_e, n_used, row_tok, row_tok, h, wgu, bgu, wdn, bdn)


def _post2_kernel(pos_ref, posn_ref, h_ref, ple_ref, gate_ref, g_ref, b_ref, y_hbm, o_ref, ybuf, sem,
                  *, alpha, tm):
    i = pl.program_id(0)
    slot = i % 2

    def gather(pos, sl, fn):
        def body(r, _):
            for k in range(TOP_K):
                fn(pltpu.make_async_copy(y_hbm.at[pl.ds(pos[0, 0, r * TOP_K + k], 1)],
                                         ybuf.at[sl, k, pl.ds(r, 1)], sem.at[sl]))
            return 0
        lax.fori_loop(0, tm, body, 0)

    @pl.when(i == 0)
    def _():
        gather(pos_ref, slot, lambda cp: cp.start())

    @pl.when(i + 1 < pl.num_programs(0))
    def _():
        gather(posn_ref, 1 - slot, lambda cp: cp.start())

    gather(pos_ref, slot, lambda cp: cp.wait())
    gate = gate_ref[...]
    ffn = gate[:, 0:1] * ybuf[slot, 0]
    for k in range(1, TOP_K):
        ffn = ffn + gate[:, k:k + 1] * ybuf[slot, k]
    o_ref[...] = _layer_norm(alpha * h_ref[...] + ffn + ple_ref[...], g_ref[...], b_ref[...])


def _post2(pos, h, ple, gate, g, b, y_rows, *, alpha, tm):
    n, d = h.shape
    nt = n // tm
    row = lambda a: pl.BlockSpec((tm, a.shape[1]), lambda i: (i, 0))
    full = lambda a: pl.BlockSpec(a.shape, lambda i: (0,) * a.ndim)
    return pl.pallas_call(
        functools.partial(_post2_kernel, alpha=alpha, tm=tm), grid=(nt,),
        in_specs=[pl.BlockSpec((1, 1, tm * TOP_K), lambda i: (i, 0, 0), memory_space=pltpu.SMEM),
                  pl.BlockSpec((1, 1, tm * TOP_K), lambda i: (jnp.minimum(i + 1, nt - 1), 0, 0),
                               memory_space=pltpu.SMEM),
                  row(h), row(ple), row(gate), full(g), full(b), pl.BlockSpec(memory_space=pl.ANY)],
        out_specs=pl.BlockSpec((tm, d), lambda i: (i, 0)),
        out_shape=jax.ShapeDtypeStruct((n, d), F32),
        scratch_shapes=[pltpu.VMEM((2, TOP_K, tm, d), F32), pltpu.SemaphoreType.DMA((2,))],
        compiler_params=_params(("arbitrary",)), name="post2",
    )(pos, pos, h, ple, gate, g, b, y_rows)


def _route(top_e, blk):
    n_tok = top_e.shape[0]
    n_asg = n_tok * TOP_K
    flat_e = top_e.reshape(n_asg)
    order = jnp.argsort(flat_e, stable=True)
    s_e = flat_e[order]
    counts = jnp.bincount(flat_e, length=N_EXP)
    starts = jnp.cumsum(counts) - counts
    padded = (counts + blk - 1) // blk * blk
    pad_ends = jnp.cumsum(padded)
    dest = (pad_ends[s_e] - padded[s_e] + jnp.arange(n_asg) - starts[s_e]).astype(I32)
    n_blocks = -(-n_asg // blk) + N_EXP
    row_tok = jnp.zeros((n_blocks * blk,), I32).at[dest].set((order // TOP_K).astype(I32))
    pos = jnp.zeros((n_asg,), I32).at[order].set(dest)
    n_used = (pad_ends[-1] // blk).astype(I32)
    blk_start = jnp.arange(n_blocks) * blk
    block_e = jnp.minimum(jnp.searchsorted(pad_ends, blk_start, side='right'), N_EXP - 1)
    last_e = block_e[jnp.maximum(n_used - 1, 0)]
    block_e = jnp.where(jnp.arange(n_blocks) < n_used, block_e, last_e).astype(I32)
    return block_e, n_used.reshape(1), row_tok.reshape(n_blocks, 1, blk), pos


def _prep_layer(w_in, mla_w_uq, mla_w_uk, mla_w_uv, router_w, router_b):
    d = w_in.shape[0]
    cols = np.concatenate([np.arange(0, 1152), np.arange(1184, 1824), np.arange(1152, 1184)])
    win = jnp.concatenate([w_in[:, cols], jnp.zeros((d, IN_W - cols.size), w_in.dtype)], axis=1).astype(BF16)
    per = MLA_NOPE + MLA_ROPE
    nope = np.concatenate([np.arange(h * per, h * per + MLA_NOPE) for h in range(MLA_H)])
    rope = np.concatenate([np.arange(h * per + MLA_NOPE, (h + 1) * per) for h in range(MLA_H)])
    ucols = np.concatenate([nope, rope])
    wuq = jnp.concatenate([mla_w_uq[:, ucols], jnp.zeros((MLA_QL, UQ_W - ucols.size), mla_w_uq.dtype)],
                          axis=1).astype(BF16)
    wuk = jnp.zeros((MLA_H * MLA_NOPE, MLA_H * MLA_R), F32)
    for h in range(MLA_H):
        wuk = wuk.at[h * MLA_NOPE:(h + 1) * MLA_NOPE, h * MLA_R:(h + 1) * MLA_R].set(mla_w_uk[:, h, :].T)
    wuv = jnp.transpose(mla_w_uv, (1, 0, 2)).astype(BF16)
    rw = jnp.concatenate([router_w, jnp.zeros((d, LANES - N_EXP), router_w.dtype)], axis=1)
    rw_hi = rw.astype(BF16)
    rw_lo = (rw - rw_hi.astype(F32)).astype(BF16)
    rb = jnp.concatenate([router_b, jnp.zeros((LANES - N_EXP,), router_b.dtype)])[None, :]
    return win, wuq, wuk.astype(BF16), wuv, rw_hi, rw_lo, rb


def _shifted(u, prev, seq):
    b = u.shape[0] // seq
    ext = jnp.concatenate([prev.astype(u.dtype), u.reshape(b, seq, -1)], axis=1)
    w = u.shape[1]
    return ext[:, 1:-1].reshape(-1, w), ext[:, :-2].reshape(-1, w), ext[:, -(CONV_K - 1):]


def kernel(x_prompt, x_sample, state_conv, cache_mla_latent, cache_mla_rope, cache_moba_k, cache_moba_v,
           page_table, p_prompt, p_sample, w_in, conv_w, mla_q_norm, mla_kv_norm, mla_w_uq, mla_w_uk,
           mla_w_uv, w_o, ln1_g, ln1_b, router_w, router_b, w_gu, b_gu, w_down, b_down, ple_w,
           ple_gate_w, ple_gate_b, ln2_g, ln2_b):
    batch, seq, d = x_prompt.shape
    dec_batch, t_new, _ = x_sample.shape
    depth = w_in.shape[0]
    n_p, n_s = batch * seq, dec_batch * t_new
    n_pages = page_table.shape[1]
    past = n_pages * PAGE
    tm = min(TOK_TILE, n_s)
    tq = min(ATT_TILE, seq)
    assert n_p % tm == 0 and n_s % tm == 0 and seq % tm == 0 and tm % t_new == 0 and seq % MOBA_BLK == 0
    alpha = (2 * depth) ** 0.25
    cache_k = cache_moba_k.reshape(cache_moba_k.shape[:3] + (LANES,))
    cache_v = cache_moba_v.reshape(cache_moba_v.shape[:3] + (LANES,))
    tab = _rope_table(seq, past, t_new, tm)
    x = jnp.concatenate([x_prompt.reshape(n_p, d), x_sample.reshape(n_s, d)], axis=0)
    outs = [[] for _ in range(10)]
    for i in range(depth):
        win, wuq, wuk, wuv, rw_hi, rw_lo, rb = _prep_layer(w_in[i], mla_w_uq[i], mla_w_uk[i], mla_w_uv[i],
                                                           router_w[i], router_b[i])
        u, cvb, qcat, kcat, ckv, kpe, mq, mk, mv, mkv = _proj(
            x, tab, win, wuq, wuk, mla_q_norm[i][None, :], mla_kv_norm[i][None, :], n_prompt=n_p, seq=seq, tm=tm)
        mla_p = _mla_prompt(qcat, kcat, wuv, batch=batch, seq=seq, tq=tq)
        moba_p = _moba_prompt(mq, mkv, mk, batch=batch, seq=seq)
        q_s = qcat[:, n_p:].reshape(MLA_H, dec_batch, t_new, MLA_KW).transpose(1, 0, 2, 3)
        q_s = q_s.reshape(dec_batch, MLA_H * t_new, MLA_KW)
        k_new = jnp.concatenate([ckv[n_p:], kpe[n_p:]], axis=1).reshape(dec_batch, t_new, MLA_KW)
        mla_s = _mla_sample(page_table, q_s, k_new, wuv, cache_mla_latent, cache_mla_rope, layer=i)
        mq_s = mq[n_p:].reshape(dec_batch, t_new, MOBA_KVH, MOBA_G, MOBA_D).transpose(0, 2, 3, 1, 4)
        mq_s = mq_s.reshape(dec_batch, MOBA_KVH, MOBA_G * t_new, MOBA_D)
        zero = jnp.zeros_like(mq_s[:, 0])
        q_bd = jnp.concatenate([jnp.concatenate([mq_s[:, 0], zero], axis=2),
                                jnp.concatenate([zero, mq_s[:, 1]], axis=2)], axis=1)
        moba_s = _moba_sample(page_table, q_bd, mk[n_p:].reshape(dec_batch, t_new, LANES),
                              mv[n_p:].reshape(dec_batch, t_new, LANES), cache_k, cache_v, layer=i)
        moba_s = moba_s.reshape(dec_batch, MOBA_KVH, MOBA_G, t_new, MOBA_KVH, MOBA_D)
        moba_s = jnp.stack([moba_s[:, g, :, :, g] for g in range(MOBA_KVH)], axis=1)
        moba_s = moba_s.transpose(0, 3, 1, 2, 4).reshape(n_s, MOBA_H * MOBA_D)
        mla_v = jnp.concatenate([mla_p, mla_s.reshape(n_s, -1).astype(BF16)], axis=0)
        moba_o = jnp.concatenate([moba_p, moba_s.astype(BF16)], axis=0)
        u1p, u2p, st_p = _shifted(u[:n_p], jnp.zeros((batch, CONV_K - 1, CONV_W), F32), seq)
        u1s, u2s, st_s = _shifted(u[n_p:], state_conv[i], t_new)
        u1 = jnp.concatenate([u1p, u1s], axis=0)
        u2 = jnp.concatenate([u2p, u2s], axis=0)
        p = jnp.concatenate([p_prompt[i].reshape(n_p, -1), p_sample[i].reshape(n_s, -1)], axis=0)
        h, ple, topi, topg = _post1(
            x, u, u1, u2, cvb, conv_w[i], mla_v, moba_o, p, w_o[i].astype(BF16), ln1_g[i][None, :], ln1_b[i][None, :],
            rw_hi, rw_lo, rb, ple_gate_w[i].astype(BF16), ple_gate_b[i][None, :], ple_w[i].astype(BF16),
            alpha=alpha, tm=tm)
        block_e, n_used, row_tok, pos = _route(topi[:, :TOP_K], MOE_TILE)
        y_rows = _moe(block_e, n_used, row_tok, h, w_gu[i].astype(BF16), b_gu[i][:, None, :],
                      w_down[i].astype(BF16), b_down[i][:, None, :], blk=MOE_TILE)
        tm2 = min(ATT_TILE, tm)
        x = _post2(pos.reshape(-1, 1, tm2 * TOP_K), h, ple, topg, ln2_g[i][None, :], ln2_b[i][None, :], y_rows,
                   alpha=alpha, tm=tm2)
        for lst, a, shp in ((outs[0], st_p, None), (outs[1], st_s, None),
                            (outs[2], ckv[:n_p], (batch, seq, MLA_R)), (outs[3], ckv[n_p:], (dec_batch, t_new, MLA_R)),
                            (outs[4], kpe[:n_p], (batch, seq, MLA_ROPE)), (outs[5], kpe[n_p:], (dec_batch, t_new, MLA_ROPE)),
                            (outs[6], mk[:n_p], (batch, seq, MOBA_KVH, MOBA_D)),
                            (outs[7], mk[n_p:], (dec_batch, t_new, MOBA_KVH, MOBA_D)),
                            (outs[8], mv[:n_p], (batch, seq, MOBA_KVH, MOBA_D)),
                            (outs[9], mv[n_p:], (dec_batch, t_new, MOBA_KVH, MOBA_D))):
            lst.append(a if shp is None else a.reshape(shp))
    y_p = x[:n_p].reshape(batch, seq, d)
    y_s = x[n_p:].reshape(dec_batch, t_new, d)
    return (y_p, y_s) + tuple(jnp.stack(o) for o in outs)
```

```python
import functools

import numpy as np
import jax
import jax.numpy as jnp
from jax import lax
from jax.experimental import pallas as pl
from jax.experimental.pallas import tpu as pltpu

F32 = jnp.float32
BF16 = jnp.bfloat16
I32 = jnp.int32

LANES = 128
SUBLANES = 8
PAGE = 128
CONV_W = 256
CONV_K = 3
MLA_H = 6
MLA_QL = 256
MLA_R = 128
MLA_NOPE = 64
MLA_ROPE = 32
MLA_V = 64
MLA_THETA = 10000.0
MLA_KW = MLA_R + MLA_ROPE
MOBA_H = 6
MOBA_KVH = 2
MOBA_G = MOBA_H // MOBA_KVH
MOBA_D = 64
MOBA_BLK = 256
MOBA_TOPK = 3
MOBA_ROT = MOBA_D // 4
MOBA_THETA = 500000.0
N_EXP = 32
TOP_K = 4
D_FF = 1024
SWIGLU_LIMIT = 7.0
SWIGLU_ALPHA = 1.702
LN_EPS = 1e-5
RMS_EPS = 1e-6
NEG = -1e30
MLA_SCALE = (MLA_NOPE + MLA_ROPE) ** -0.5
MOBA_SCALE = MOBA_D ** -0.5

OFF_H, OFF_B, OFF_C, OFF_QL, OFF_KV, OFF_MQ, OFF_MK, OFF_MV, OFF_KPE = 0, 256, 512, 768, 1024, 1152, 1536, 1664, 1792
IN_W = 1920
UQ_W = 640

TOK_TILE = 512
ATT_TILE = 256
MOE_TILE = 256
PAGES_PER_CHUNK = 16
VMEM_LIMIT = 56 * 1024 * 1024


def _nt(a, b):
    return lax.dot_general(a, b, (((1,), (1,)), ((), ())), preferred_element_type=F32)


def _dot(a, b):
    return jnp.dot(a, b, preferred_element_type=F32)


def _params(sem):
    return pltpu.CompilerParams(dimension_semantics=sem, vmem_limit_bytes=VMEM_LIMIT)


def _rope_lane_tables(pos, dim, theta, period):
    half = dim // 2
    inv = theta ** (-jnp.arange(0, dim, 2, dtype=F32) / dim)
    ang = pos.astype(F32)[:, None] * inv[None, :]
    cos, sin = jnp.cos(ang), jnp.sin(ang)
    d = np.arange(LANES) % period
    j = d % half
    rot = jnp.asarray(d < dim)[None, :]
    first = jnp.asarray(d < half)[None, :]
    c = jnp.where(rot, cos[:, j], 1.0)
    s1 = jnp.where(rot & first, -sin[:, j], 0.0)
    s2 = jnp.where(rot & ~first, sin[:, j], 0.0)
    return c, s1, s2


def _rope_table(seq, past, t_new, tile):
    pos = jnp.concatenate([jnp.arange(seq, dtype=I32), past + (jnp.arange(tile, dtype=I32) % t_new)])
    mla = _rope_lane_tables(pos, MLA_ROPE, MLA_THETA, MLA_ROPE)
    moba = _rope_lane_tables(pos, MOBA_ROT, MOBA_THETA, MOBA_D)
    return jnp.concatenate(list(mla) + list(moba), axis=1)


def _rope(x, tab, base, half):
    c = tab[:, base:base + LANES]
    s1 = tab[:, base + LANES:base + 2 * LANES]
    s2 = tab[:, base + 2 * LANES:base + 3 * LANES]
    return x * c + pltpu.roll(x, LANES - half, 1) * s1 + pltpu.roll(x, half, 1) * s2


def _proj_kernel(x_ref, tab_ref, win_ref, wuq_ref, wuk_ref, qn_ref, kvn_ref,
                 u_ref, cvb_ref, qcat_ref, kcat_ref, ckv_ref, kpe_ref, mq_ref, mk_ref, mv_ref, mkv_ref):
    y = _dot(x_ref[...].astype(BF16), win_ref[...])
    tab = tab_ref[...]
    u_ref[...] = y[:, OFF_C:OFF_C + CONV_W] * y[:, OFF_H:OFF_H + CONV_W]
    cvb_ref[...] = y[:, OFF_B:OFF_B + CONV_W]
    q_lat = y[:, OFF_QL:OFF_QL + MLA_QL]
    q_n = q_lat * lax.rsqrt(jnp.mean(q_lat * q_lat, axis=1, keepdims=True) + RMS_EPS) * qn_ref[...]
    q = _dot(q_n.astype(BF16), wuq_ref[...])
    n_nope = MLA_H * MLA_NOPE
    q_abs = _dot(q[:, :n_nope].astype(BF16), wuk_ref[...]) * MLA_SCALE
    q_pe = [_rope(q[:, n_nope + k * LANES:n_nope + (k + 1) * LANES], tab, 0, MLA_ROPE // 2) * MLA_SCALE
            for k in range(2)]
    per_slab = LANES // MLA_ROPE
    for h in range(MLA_H):
        qcat_ref[h, :, 0:MLA_R] = q_abs[:, h * MLA_R:(h + 1) * MLA_R].astype(BF16)
        lo = (h % per_slab) * MLA_ROPE
        qcat_ref[h, :, MLA_R:MLA_KW] = q_pe[h // per_slab][:, lo:lo + MLA_ROPE].astype(BF16)
    kv_lat = y[:, OFF_KV:OFF_KV + MLA_R]
    c_kv = kv_lat * lax.rsqrt(jnp.mean(kv_lat * kv_lat, axis=1, keepdims=True) + RMS_EPS) * kvn_ref[...]
    k_pe = _rope(y[:, OFF_KPE:OFF_KPE + LANES], tab, 0, MLA_ROPE // 2)[:, :MLA_ROPE]
    ckv_ref[...] = c_kv
    kpe_ref[...] = k_pe
    kcat_ref[:, 0:MLA_R] = c_kv.astype(BF16)
    kcat_ref[:, MLA_R:MLA_KW] = k_pe.astype(BF16)
    for k in range(MOBA_H * MOBA_D // LANES):
        slab = _rope(y[:, OFF_MQ + k * LANES:OFF_MQ + (k + 1) * LANES], tab, 3 * LANES, MOBA_ROT // 2)
        mq_ref[:, k * LANES:(k + 1) * LANES] = (slab * MOBA_SCALE).astype(BF16)
    mk = _rope(y[:, OFF_MK:OFF_MK + LANES], tab, 3 * LANES, MOBA_ROT // 2)
    mv = y[:, OFF_MV:OFF_MV + LANES]
    mk_ref[...] = mk
    mv_ref[...] = mv
    mkv_ref[:, 0:LANES] = mk.astype(BF16)
    mkv_ref[:, LANES:2 * LANES] = mv.astype(BF16)


def _proj(x, tab, win, wuq, wuk, qn, kvn, *, n_prompt, seq, tm):
    n = x.shape[0]
    n_pt = n_prompt // tm
    per_seq = seq // tm

    def tab_map(i):
        return (jnp.where(i < n_pt, i % per_seq, per_seq), 0)

    row = lambda w: pl.BlockSpec((tm, w), lambda i: (i, 0))
    full = lambda a: pl.BlockSpec(a.shape, lambda i: (0,) * a.ndim)
    out_shape = [
        jax.ShapeDtypeStruct((n, CONV_W), F32), jax.ShapeDtypeStruct((n, CONV_W), F32),
        jax.ShapeDtypeStruct((MLA_H, n, MLA_KW), BF16), jax.ShapeDtypeStruct((n, MLA_KW), BF16),
        jax.ShapeDtypeStruct((n, MLA_R), F32), jax.ShapeDtypeStruct((n, MLA_ROPE), F32),
        jax.ShapeDtypeStruct((n, MOBA_H * MOBA_D), BF16), jax.ShapeDtypeStruct((n, LANES), F32),
        jax.ShapeDtypeStruct((n, LANES), F32), jax.ShapeDtypeStruct((n, 2 * LANES), BF16),
    ]
    out_specs = [
        row(CONV_W), row(CONV_W), pl.BlockSpec((MLA_H, tm, MLA_KW), lambda i: (0, i, 0)), row(MLA_KW),
        row(MLA_R), row(MLA_ROPE), row(MOBA_H * MOBA_D), row(LANES), row(LANES), row(2 * LANES),
    ]
    return pl.pallas_call(
        _proj_kernel, grid=(n // tm,),
        in_specs=[row(x.shape[1]), pl.BlockSpec((tm, tab.shape[1]), tab_map),
                  full(win), full(wuq), full(wuk), full(qn), full(kvn)],
        out_specs=out_specs, out_shape=out_shape,
        compiler_params=_params(("arbitrary",)), name="proj",
    )(x, tab, win, wuq, wuk, qn, kvn)


def _softmax_update(carry, s, v):
    m, l, acc = carry
    m_new = jnp.maximum(m, jnp.max(s, axis=1, keepdims=True))
    p = jnp.exp(s - m_new)
    corr = jnp.exp(m - m_new)
    l = l * corr + jnp.sum(p, axis=1, keepdims=True)
    acc = acc * corr + _dot(p.astype(BF16), v)
    return m_new, l, acc


def _softmax_init(rows, width):
    return (jnp.full((rows, 1), NEG, F32), jnp.zeros((rows, 1), F32), jnp.zeros((rows, width), F32))


def _causal_mask(rows, t_q, cols):
    r = lax.broadcasted_iota(I32, (rows, cols), 0) % t_q
    c = lax.broadcasted_iota(I32, (rows, cols), 1)
    return c <= r


def _mla_prompt_kernel(q_ref, k_ref, wuv_ref, o_ref, *, tq):
    i = pl.program_id(1)
    rows = MLA_H * tq
    q = q_ref[...].reshape(rows, MLA_KW)

    def tile(j):
        k = k_ref[pl.ds(pl.multiple_of(j * tq, tq), tq), :]
        return _nt(q, k), k[:, :MLA_R]

    s, v = tile(i)
    s = jnp.where(_causal_mask(rows, tq, tq), s, NEG)
    carry = _softmax_update(_softmax_init(rows, MLA_R), s, v)
    carry = lax.fori_loop(0, i, lambda j, c: _softmax_update(c, *tile(j)), carry)
    _, l, acc = carry
    o = acc / l
    outs = [_dot(o[h * tq:(h + 1) * tq].astype(BF16), wuv_ref[h]) for h in range(MLA_H)]
    o_ref[...] = jnp.concatenate(outs, axis=1).astype(o_ref.dtype)


def _mla_prompt(qcat, kcat, wuv, *, batch, seq, tq):
    nq = seq // tq
    return pl.pallas_call(
        functools.partial(_mla_prompt_kernel, tq=tq), grid=(batch, nq),
        in_specs=[pl.BlockSpec((MLA_H, tq, MLA_KW), lambda b, i: (0, b * nq + i, 0)),
                  pl.BlockSpec((seq, MLA_KW), lambda b, i: (b, 0)),
                  pl.BlockSpec(wuv.shape, lambda b, i: (0, 0, 0))],
        out_specs=pl.BlockSpec((tq, MLA_H * MLA_V), lambda b, i: (b * nq + i, 0)),
        out_shape=jax.ShapeDtypeStruct((batch * seq, MLA_H * MLA_V), BF16),
        compiler_params=_params(("arbitrary", "arbitrary")), name="mla_prompt",
    )(qcat, kcat, wuv)


def _moba_prompt_kernel(q_ref, kv_ref, kf_ref, o_ref, km_ref, *, nb):
    i = pl.program_id(1)
    tq = MOBA_BLK
    rows = MOBA_G * tq

    @pl.when(i == 0)
    def _():
        km_ref[...] = jnp.zeros(km_ref.shape, F32)
        for n in range(nb):
            km_ref[n:n + 1, :] = jnp.mean(kf_ref[n * tq:(n + 1) * tq, :], axis=0, keepdims=True)

    q_all = q_ref[...]
    lane = lax.broadcasted_iota(I32, (rows, LANES), 1)
    blk_row = lax.broadcasted_iota(I32, (LANES, tq), 0)
    causal = _causal_mask(rows, tq, tq)
    pieces = []
    for g in range(MOBA_KVH):
        qg = jnp.concatenate([q_all[:, (MOBA_G * g + hh) * MOBA_D:(MOBA_G * g + hh + 1) * MOBA_D]
                              for hh in range(MOBA_G)], axis=0)
        gate = _nt(qg, km_ref[:, g * MOBA_D:(g + 1) * MOBA_D].astype(BF16))
        gate = jnp.where(lane < i, gate, NEG)
        cnt = jnp.zeros((rows, LANES), I32)
        for m in range(nb - 1):
            col = gate[:, m:m + 1]
            beats = (col > gate) | ((col == gate) & (lane > m))
            cnt = cnt + beats.astype(I32)
        sel = jnp.where((lane < i) & (cnt < MOBA_TOPK), 1.0, 0.0).astype(BF16)

        def kv_tile(n):
            t = kv_ref[pl.ds(pl.multiple_of(n * tq, tq), tq), :]
            return t[:, g * MOBA_D:(g + 1) * MOBA_D], t[:, LANES + g * MOBA_D:LANES + (g + 1) * MOBA_D]

        k, v = kv_tile(i)
        s = jnp.where(causal, _nt(qg, k), NEG)
        carry = _softmax_update(_softmax_init(rows, MOBA_D), s, v)

        def past(n, c):
            k, v = kv_tile(n)
            picked = _dot(sel, (blk_row == n).astype(BF16))
            return _softmax_update(c, jnp.where(picked > 0.5, _nt(qg, k), NEG), v)

        _, l, acc = lax.fori_loop(0, i, past, carry)
        o = acc / l
        pieces += [o[hh * tq:(hh + 1) * tq] for hh in range(MOBA_G)]
    o_ref[...] = jnp.concatenate(pieces, axis=1).astype(o_ref.dtype)


def _moba_prompt(mq, mkv, mk, *, batch, seq):
    nb = seq // MOBA_BLK
    return pl.pallas_call(
        functools.partial(_moba_prompt_kernel, nb=nb), grid=(batch, nb),
        in_specs=[pl.BlockSpec((MOBA_BLK, MOBA_H * MOBA_D), lambda b, i: (b * nb + i, 0)),
                  pl.BlockSpec((seq, 2 * LANES), lambda b, i: (b, 0)),
                  pl.BlockSpec((seq, LANES), lambda b, i: (b, 0))],
        out_specs=pl.BlockSpec((MOBA_BLK, MOBA_H * MOBA_D), lambda b, i: (b * nb + i, 0)),
        out_shape=jax.ShapeDtypeStruct((batch * seq, MOBA_H * MOBA_D), BF16),
        scratch_shapes=[pltpu.VMEM((LANES, LANES), F32)],
        compiler_params=_params(("arbitrary", "arbitrary")), name="moba_prompt",
    )(mq, mkv, mk)


def _mla_sample_kernel(pt_ref, q_ref, knew_ref, wuv_ref, lat_hbm, ropet_hbm, o_ref,
                       lat_buf, rope_buf, lat_sem, rope_sem, latb_ref, s_ref, *, layer, ppc, nchunk, t_new):
    b = pl.program_id(0)
    rows = MLA_H * t_new
    keys = ppc * PAGE

    def copies(bb, chunk, slot):
        out = []
        for j in range(ppc):
            pg = pt_ref[bb, chunk * ppc + j]
            out.append(pltpu.make_async_copy(lat_hbm.at[layer, pg], lat_buf.at[slot, j], lat_sem.at[slot, j]))
            out.append(pltpu.make_async_copy(ropet_hbm.at[layer, pg], rope_buf.at[slot, :, pl.ds(j * PAGE, PAGE)],
                                             rope_sem.at[slot, j]))
        return out

    def start(bb, chunk, slot):
        for cp in copies(bb, chunk, slot):
            cp.start()

    @pl.when(b == 0)
    def _():
        start(b, 0, 0)

    q = q_ref[0]
    q_abs, q_pe = q[:, :MLA_R], q[:, MLA_R:]
    for c in range(nchunk):
        slot = c % 2
        if c + 1 < nchunk:
            start(b, c + 1, 1 - slot)
        else:
            pl.when(b + 1 < pl.num_programs(0))(functools.partial(start, b + 1, 0, 1 - slot))
        for cp in copies(b, c, slot):
            cp.wait()
        lat = lat_buf[slot].reshape(keys, MLA_R).astype(BF16)
        latb_ref[c * keys:(c + 1) * keys, :] = lat
        s_ref[:, c * keys:(c + 1) * keys] = _nt(q_abs, lat) + _dot(q_pe, rope_buf[slot].astype(BF16))

    k_new = knew_ref[0].astype(BF16)
    s_new = jnp.where(_causal_mask(rows, t_new, t_new), _nt(q, k_new), NEG)
    s_all = s_ref[...]
    m = jnp.maximum(jnp.max(s_all, axis=1, keepdims=True), jnp.max(s_new, axis=1, keepdims=True))
    p = jnp.exp(s_all - m)
    p_new = jnp.exp(s_new - m)
    l = jnp.sum(p, axis=1, keepdims=True) + jnp.sum(p_new, axis=1, keepdims=True)
    acc = _dot(p.astype(BF16), latb_ref[...]) + _dot(p_new.astype(BF16), k_new[:, :MLA_R])
    o = acc / l
    outs = [_dot(o[h * t_new:(h + 1) * t_new].astype(BF16), wuv_ref[h]) for h in range(MLA_H)]
    o_ref[0] = jnp.concatenate(outs, axis=1)


def _mla_sample(page_table, q, k_new, wuv, cache_lat, cache_ropet, *, layer):
    nbatch, n_pages = page_table.shape
    t_new = k_new.shape[1]
    ppc = min(PAGES_PER_CHUNK, n_pages)
    nchunk = n_pages // ppc
    assert n_pages % ppc == 0 and nchunk % 2 == 0
    rows = MLA_H * t_new
    past = n_pages * PAGE
    grid_spec = pltpu.PrefetchScalarGridSpec(
        num_scalar_prefetch=1, grid=(nbatch,),
        in_specs=[pl.BlockSpec((1, rows, MLA_KW), lambda b, pt: (b, 0, 0)),
                  pl.BlockSpec((1, t_new, MLA_KW), lambda b, pt: (b, 0, 0)),
                  pl.BlockSpec(wuv.shape, lambda b, pt: (0, 0, 0)),
                  pl.BlockSpec(memory_space=pl.ANY), pl.BlockSpec(memory_space=pl.ANY)],
        out_specs=pl.BlockSpec((1, t_new, MLA_H * MLA_V), lambda b, pt: (b, 0, 0)),
        scratch_shapes=[pltpu.VMEM((2, ppc, PAGE, MLA_R), F32), pltpu.VMEM((2, MLA_ROPE, ppc * PAGE), F32),
                        pltpu.SemaphoreType.DMA((2, ppc)), pltpu.SemaphoreType.DMA((2, ppc)),
                        pltpu.VMEM((past, MLA_R), BF16), pltpu.VMEM((rows, past), F32)])
    return pl.pallas_call(
        functools.partial(_mla_sample_kernel, layer=layer, ppc=ppc, nchunk=nchunk, t_new=t_new),
        grid_spec=grid_spec, out_shape=jax.ShapeDtypeStruct((nbatch, t_new, MLA_H * MLA_V), F32),
        compiler_params=_params(("arbitrary",)), name="mla_sample",
    )(page_table, q, k_new, wuv, cache_lat, cache_ropet)


def _moba_sample_kernel(pt_ref, q_ref, knew_ref, vnew_ref, kt_hbm, vt_hbm, o_ref,
                        buf, sem, s_ref, p_ref, gate_ref, *, layer, ppc, nchunk, t_new, n_full):
    b = pl.program_id(0)
    rows = q_ref.shape[1]
    keys = ppc * PAGE
    bpc = keys // MOBA_BLK
    srcs = (kt_hbm, vt_hbm)

    def copies(bb, step, slot):
        src, chunk = srcs[step // nchunk], step % nchunk
        return [pltpu.make_async_copy(src.at[layer, pt_ref[bb, chunk * ppc + j]],
                                      buf.at[slot, :, pl.ds(j * PAGE, PAGE)], sem.at[slot, j]) for j in range(ppc)]

    def start(bb, step, slot):
        for cp in copies(bb, step, slot):
            cp.start()

    def fetch(step):
        slot = step % 2
        if step + 1 < 2 * nchunk:
            start(b, step + 1, 1 - slot)
        else:
            pl.when(b + 1 < pl.num_programs(0))(functools.partial(start, b + 1, 0, 1 - slot))
        for cp in copies(b, step, slot):
            cp.wait()
        return buf[slot].astype(BF16)

    @pl.when(b == 0)
    def _():
        start(b, 0, 0)

    q = q_ref[0]
    lane = lax.broadcasted_iota(I32, (rows, LANES), 1)
    gate_ref[...] = jnp.full(gate_ref.shape, NEG, F32)
    for c in range(nchunk):
        s = _dot(q, fetch(c))
        s_ref[:, c * keys:(c + 1) * keys] = s
        for n in range(bpc):
            blk = c * bpc + n
            gate_ref[:, blk:blk + 1] = jnp.sum(s[:, n * MOBA_BLK:(n + 1) * MOBA_BLK], axis=1, keepdims=True)

    gate = jnp.where(lane < n_full, gate_ref[...], NEG)
    sel = jnp.zeros((rows, LANES), F32)
    for _ in range(MOBA_TOPK):
        best = jnp.max(gate, axis=1, keepdims=True)
        first = jnp.min(jnp.where(gate == best, lane, LANES), axis=1, keepdims=True)
        hit = lane == first
        sel = jnp.where(hit, 1.0, sel)
        gate = jnp.where(hit, NEG * 2, gate)
    sel = sel.astype(BF16)

    def picked(c):
        blk = lax.broadcasted_iota(I32, (LANES, keys), 0)
        key_blk = c * bpc + lax.broadcasted_iota(I32, (LANES, keys), 1) // MOBA_BLK
        return _dot(sel, (blk == key_blk).astype(BF16)) > 0.5

    new_mask = _causal_mask(rows, t_new, t_new)
    s_new = jnp.where(new_mask, _nt(q, knew_ref[0].astype(BF16)), NEG)
    m = jnp.max(s_new, axis=1, keepdims=True)
    for c in range(nchunk):
        m = jnp.maximum(m, jnp.max(jnp.where(picked(c), s_ref[:, c * keys:(c + 1) * keys], NEG), axis=1, keepdims=True))
    p_new = jnp.where(new_mask, jnp.exp(s_new - m), 0.0)
    l = jnp.sum(p_new, axis=1, keepdims=True)
    for c in range(nchunk):
        p = jnp.where(picked(c), jnp.exp(s_ref[:, c * keys:(c + 1) * keys] - m), 0.0)
        l = l + jnp.sum(p, axis=1, keepdims=True)
        p_ref[:, c * keys:(c + 1) * keys] = p.astype(BF16)

    acc = _dot(p_new.astype(BF16), vnew_ref[0].astype(BF16))
    for c in range(nchunk):
        acc = acc + _nt(p_ref[:, c * keys:(c + 1) * keys], fetch(nchunk + c))
    o_ref[0] = acc / l


def _moba_sample(page_table, q_bd, k_new, v_new, cache_kt, cache_vt, *, layer):
    nbatch, n_pages = page_table.shape
    t_new = k_new.shape[1]
    rows = q_bd.shape[1]
    ppc = min(PAGES_PER_CHUNK, n_pages)
    keys = ppc * PAGE
    past = n_pages * PAGE
    assert n_pages % ppc == 0 and keys % MOBA_BLK == 0 and past % MOBA_BLK == 0
    n_full = past // MOBA_BLK
    assert MOBA_TOPK <= n_full <= LANES
    nchunk = n_pages // ppc
    grid_spec = pltpu.PrefetchScalarGridSpec(
        num_scalar_prefetch=1, grid=(nbatch,),
        in_specs=[pl.BlockSpec((1, rows, LANES), lambda b, pt: (b, 0, 0)),
                  pl.BlockSpec((1, t_new, LANES), lambda b, pt: (b, 0, 0)),
                  pl.BlockSpec((1, t_new, LANES), lambda b, pt: (b, 0, 0)),
                  pl.BlockSpec(memory_space=pl.ANY), pl.BlockSpec(memory_space=pl.ANY)],
        out_specs=pl.BlockSpec((1, rows, LANES), lambda b, pt: (b, 0, 0)),
        scratch_shapes=[pltpu.VMEM((2, LANES, keys), F32), pltpu.SemaphoreType.DMA((2, ppc)),
                        pltpu.VMEM((rows, past), F32), pltpu.VMEM((rows, past), BF16),
                        pltpu.VMEM((rows, LANES), F32)])
    return pl.pallas_call(
        functools.partial(_moba_sample_kernel, layer=layer, ppc=ppc, nchunk=nchunk, t_new=t_new, n_full=n_full),
        grid_spec=grid_spec, out_shape=jax.ShapeDtypeStruct((nbatch, rows, LANES), F32),
        compiler_params=_params(("arbitrary",)), name="moba_sample",
    )(page_table, q_bd, k_new, v_new, cache_kt, cache_vt)


def _layer_norm(x, g, b):
    mu = jnp.mean(x, axis=1, keepdims=True)
    xc = x - mu
    var = jnp.mean(xc * xc, axis=1, keepdims=True)
    return xc * lax.rsqrt(var + LN_EPS) * g + b


def _post1_kernel(x_ref, u_ref, u1_ref, u2_ref, cvb_ref, cw_ref, mla_ref, moba_ref, p_ref,
                  wo_ref, g_ref, b_ref, rwh_ref, rwl_ref, rb_ref, gw_ref, gb_ref, pw_ref,
                  h_ref, ple_ref, topi_ref, topg_ref, rank_ref, cnt_ref, run_ref, *, alpha):
    cw = cw_ref[...]
    conv = cvb_ref[...] * (cw[0:1] * u2_ref[...] + cw[1:2] * u1_ref[...] + cw[2:3] * u_ref[...])
    mix_in = jnp.concatenate([conv.astype(BF16), mla_ref[...], moba_ref[...]], axis=1)
    h = _layer_norm(alpha * x_ref[...] + _dot(mix_in, wo_ref[...]), g_ref[...], b_ref[...])
    h_ref[...] = h
    h_hi = h.astype(BF16)
    h_lo = (h - h_hi.astype(F32)).astype(BF16)
    ple_ref[...] = jax.nn.sigmoid(_dot(h_hi, gw_ref[...]) + gb_ref[...]) * _dot(p_ref[...].astype(BF16), pw_ref[...])
    logits = _dot(h_hi, rwh_ref[...]) + (_dot(h_lo, rwh_ref[...]) + _dot(h_hi, rwl_ref[...])) + rb_ref[...]
    tm = logits.shape[0]
    lane = lax.broadcasted_iota(I32, logits.shape, 1)
    logits = jnp.where(lane < N_EXP, logits, NEG)
    topi = jnp.zeros(logits.shape, I32)
    topv = jnp.zeros(logits.shape, F32)
    chosen = jnp.zeros(logits.shape, F32)
    picks = []
    v0 = None
    den = jnp.zeros((tm, 1), F32)
    for k in range(TOP_K):
        best = jnp.max(logits, axis=1, keepdims=True)
        first = jnp.min(jnp.where(logits == best, lane, LANES), axis=1, keepdims=True)
        v0 = best if v0 is None else v0
        e = jnp.exp(best - v0)
        den = den + e
        hit = lane == first
        picks.append(hit)
        chosen = jnp.where(hit, 1.0, chosen)
        topi = jnp.where(lane == k, first, topi)
        topv = jnp.where(lane == k, e, topv)
        logits = jnp.where(hit, NEG * 2, logits)
    topi_ref[...] = topi
    topg_ref[...] = topv / den

    @pl.when(pl.program_id(0) == 0)
    def _():
        run_ref[...] = jnp.zeros(run_ref.shape, F32)

    tri = lax.broadcasted_iota(I32, (tm, tm), 0) > lax.broadcasted_iota(I32, (tm, tm), 1)
    before = _dot(tri.astype(BF16), chosen.astype(BF16)) + run_ref[0:1, :]
    rank = jnp.zeros(logits.shape, I32)
    for k in range(TOP_K):
        r_k = jnp.sum(jnp.where(picks[k], before, 0.0), axis=1, keepdims=True)
        rank = jnp.where(lane == k, r_k.astype(I32), rank)
    rank_ref[...] = rank
    run = run_ref[...] + jnp.sum(chosen, axis=0, keepdims=True)
    run_ref[...] = run
    cnt_ref[0] = run


def _post1(x, u, u1, u2, cvb, cw, mla_v, moba_o, p, wo, g, b, rwh, rwl, rb, gw, gb, pw, *, alpha, tm):
    n, d = x.shape
    nt = n // tm
    row = lambda a: pl.BlockSpec((tm, a.shape[1]), lambda i: (i, 0))
    full = lambda a: pl.BlockSpec(a.shape, lambda i: (0,) * a.ndim)
    ins = (x, u, u1, u2, cvb, cw, mla_v, moba_o, p, wo, g, b, rwh, rwl, rb, gw, gb, pw)
    is_row = (1, 1, 1, 1, 1, 0, 1, 1, 1, 0, 0, 0, 0, 0, 0, 0, 0, 0)
    out_shape = [jax.ShapeDtypeStruct((n, d), F32), jax.ShapeDtypeStruct((n, d), F32),
                 jax.ShapeDtypeStruct((n, LANES), I32), jax.ShapeDtypeStruct((n, LANES), F32),
                 jax.ShapeDtypeStruct((n, LANES), I32), jax.ShapeDtypeStruct((nt, SUBLANES, LANES), F32)]
    out_specs = [pl.BlockSpec((tm, s.shape[1]), lambda i: (i, 0)) for s in out_shape[:5]]
    out_specs.append(pl.BlockSpec((1, SUBLANES, LANES), lambda i: (i, 0, 0)))
    return pl.pallas_call(
        functools.partial(_post1_kernel, alpha=alpha), grid=(nt,),
        in_specs=[row(a) if r else full(a) for a, r in zip(ins, is_row)],
        out_specs=out_specs, out_shape=out_shape,
        scratch_shapes=[pltpu.VMEM((SUBLANES, LANES), F32)],
        compiler_params=_params(("arbitrary",)), name="post1",
    )(*ins)


def _moe_kernel(be_ref, tok_ref, tokn_ref, h_hbm, wgu_ref, bgu_ref, wdn_ref, bdn_ref, y_ref, xbuf, sem, *, blk):
    i = pl.program_id(0)
    last = pl.num_programs(0) - 1
    slot = i % 2

    def gather(tok, sl):
        for r in range(blk):
            pltpu.make_async_copy(h_hbm.at[pl.ds(tok[0, 0, r], 1)], xbuf.at[sl, pl.ds(r, 1)], sem.at[sl]).start()

    def drain(sl):
        pltpu.make_async_copy(h_hbm.at[pl.ds(0, blk)], xbuf.at[sl], sem.at[sl]).wait()

    @pl.when(i == 0)
    def _():
        gather(tok_ref, slot)

    gather(tokn_ref, 1 - slot)
    drain(slot)
    x = xbuf[slot].astype(BF16)
    hgu = _dot(x, wgu_ref[0]) + bgu_ref[0]
    g = jnp.minimum(hgu[:, :D_FF], SWIGLU_LIMIT)
    u = jnp.clip(hgu[:, D_FF:], -SWIGLU_LIMIT, SWIGLU_LIMIT)
    act = (u + 1.0) * (g * jax.nn.sigmoid(SWIGLU_ALPHA * g))
    y_ref[...] = _dot(act.astype(BF16), wdn_ref[0]) + bdn_ref[0]

    @pl.when(i == last)
    def _():
        drain(1 - slot)


def _moe(block_e, row_tok, h, wgu, bgu, wdn, bdn, *, blk):
    n_blocks = row_tok.shape[0]
    d = h.shape[1]
    grid_spec = pltpu.PrefetchScalarGridSpec(
        num_scalar_prefetch=1, grid=(n_blocks,),
        in_specs=[pl.BlockSpec((1, 1, blk), lambda i, be: (i, 0, 0), memory_space=pltpu.SMEM),
                  pl.BlockSpec((1, 1, blk), lambda i, be: (jnp.minimum(i + 1, n_blocks - 1), 0, 0),
                               memory_space=pltpu.SMEM),
                  pl.BlockSpec(memory_space=pl.ANY),
                  pl.BlockSpec((1,) + wgu.shape[1:], lambda i, be: (be[i], 0, 0)),
                  pl.BlockSpec((1,) + bgu.shape[1:], lambda i, be: (be[i], 0, 0)),
                  pl.BlockSpec((1,) + wdn.shape[1:], lambda i, be: (be[i], 0, 0)),
                  pl.BlockSpec((1,) + bdn.shape[1:], lambda i, be: (be[i], 0, 0))],
        out_specs=pl.BlockSpec((blk, d), lambda i, be: (i, 0)),
        scratch_shapes=[pltpu.VMEM((2, blk, d), F32), pltpu.SemaphoreType.DMA((2,))])
    return pl.pallas_call(
        functools.partial(_moe_kernel, blk=blk), grid_spec=grid_spec,
        out_shape=jax.ShapeDtypeStruct((n_blocks * blk, d), F32),
        compiler_params=_params(("arbitrary",)), name="moe",
    )(block_e, row_tok, row_tok, h, wgu, bgu, wdn, bdn)


def _post2_kernel(pos_ref, posn_ref, h_ref, ple_ref, gate_ref, g_ref, b_ref, y_hbm, o_ref, ybuf, sem,
                  *, alpha, tm):
    i = pl.program_id(0)
    last = pl.num_programs(0) - 1
    slot = i % 2

    def gather(pos, sl):
        for r in range(tm):
            for k in range(TOP_K):
                pltpu.make_async_copy(y_hbm.at[pl.ds(pos[0, 0, r * TOP_K + k], 1)],
                                      ybuf.at[sl, k, pl.ds(r, 1)], sem.at[sl]).start()

    def drain(sl):
        for k in range(TOP_K):
            pltpu.make_async_copy(y_hbm.at[pl.ds(0, tm)], ybuf.at[sl, k], sem.at[sl]).wait()

    @pl.when(i == 0)
    def _():
        gather(pos_ref, slot)

    gather(posn_ref, 1 - slot)
    drain(slot)
    gate = gate_ref[...]
    ffn = gate[:, 0:1] * ybuf[slot, 0]
    for k in range(1, TOP_K):
        ffn = ffn + gate[:, k:k + 1] * ybuf[slot, k]
    o_ref[...] = _layer_norm(alpha * h_ref[...] + ffn + ple_ref[...], g_ref[...], b_ref[...])

    @pl.when(i == last)
    def _():
        drain(1 - slot)


def _post2(pos, h, ple, gate, g, b, y_rows, *, alpha, tm):
    n, d = h.shape
    nt = n // tm
    row = lambda a: pl.BlockSpec((tm, a.shape[1]), lambda i: (i, 0))
    full = lambda a: pl.BlockSpec(a.shape, lambda i: (0,) * a.ndim)
    return pl.pallas_call(
        functools.partial(_post2_kernel, alpha=alpha, tm=tm), grid=(nt,),
        in_specs=[pl.BlockSpec((1, 1, tm * TOP_K), lambda i: (i, 0, 0), memory_space=pltpu.SMEM),
                  pl.BlockSpec((1, 1, tm * TOP_K), lambda i: (jnp.minimum(i + 1, nt - 1), 0, 0),
                               memory_space=pltpu.SMEM),
                  row(h), row(ple), row(gate), full(g), full(b), pl.BlockSpec(memory_space=pl.ANY)],
        out_specs=pl.BlockSpec((tm, d), lambda i: (i, 0)),
        out_shape=jax.ShapeDtypeStruct((n, d), F32),
        scratch_shapes=[pltpu.VMEM((2, TOP_K, tm, d), F32), pltpu.SemaphoreType.DMA((2,))],
        compiler_params=_params(("arbitrary",)), name="post2",
    )(pos, pos, h, ple, gate, g, b, y_rows)


def _route(top_e, rank, counts, blk):
    n_tok = top_e.shape[0]
    n_asg = n_tok * TOP_K
    starts = jnp.cumsum(counts) - counts
    padded = (counts + blk - 1) // blk * blk
    pad_ends = jnp.cumsum(padded)
    pad_start = pad_ends - padded
    pos = (pad_start[top_e] + rank).reshape(n_asg).astype(I32)
    n_blocks = -(-n_asg // blk) + N_EXP
    n_used = pad_ends[-1] // blk
    block_e = jnp.minimum(jnp.searchsorted(pad_ends, jnp.arange(n_blocks, dtype=I32) * blk, side='right'), N_EXP - 1)
    order = jnp.argsort(top_e.reshape(n_asg), stable=True)
    e_r = jnp.repeat(block_e, blk)
    off = jnp.arange(n_blocks * blk, dtype=I32) - pad_start[e_r]
    src = jnp.clip(starts[e_r] + off, 0, n_asg - 1)
    row_tok = jnp.where(off < counts[e_r], order[src] // TOP_K, 0).astype(I32)
    last_e = block_e[jnp.maximum(n_used - 1, 0)]
    block_e = jnp.where(jnp.arange(n_blocks) < n_used, block_e, last_e).astype(I32)
    return block_e, row_tok.reshape(n_blocks, 1, blk), pos


def _prep_layer(w_in, mla_w_uq, mla_w_uk, mla_w_uv, router_w, router_b):
    d = w_in.shape[0]
    cols = np.concatenate([np.arange(0, 1152), np.arange(1184, 1824), np.arange(1152, 1184)])
    win = jnp.concatenate([w_in[:, cols], jnp.zeros((d, IN_W - cols.size), w_in.dtype)], axis=1).astype(BF16)
    per = MLA_NOPE + MLA_ROPE
    nope = np.concatenate([np.arange(h * per, h * per + MLA_NOPE) for h in range(MLA_H)])
    rope = np.concatenate([np.arange(h * per + MLA_NOPE, (h + 1) * per) for h in range(MLA_H)])
    ucols = np.concatenate([nope, rope])
    wuq = jnp.concatenate([mla_w_uq[:, ucols], jnp.zeros((MLA_QL, UQ_W - ucols.size), mla_w_uq.dtype)],
                          axis=1).astype(BF16)
    wuk = jnp.zeros((MLA_H * MLA_NOPE, MLA_H * MLA_R), F32)
    for h in range(MLA_H):
        wuk = wuk.at[h * MLA_NOPE:(h + 1) * MLA_NOPE, h * MLA_R:(h + 1) * MLA_R].set(mla_w_uk[:, h, :].T)
    wuv = jnp.transpose(mla_w_uv, (1, 0, 2)).astype(BF16)
    rw = jnp.concatenate([router_w, jnp.zeros((d, LANES - N_EXP), router_w.dtype)], axis=1)
    rw_hi = rw.astype(BF16)
    rw_lo = (rw - rw_hi.astype(F32)).astype(BF16)
    rb = jnp.concatenate([router_b, jnp.zeros((LANES - N_EXP,), router_b.dtype)])[None, :]
    return win, wuq, wuk.astype(BF16), wuv, rw_hi, rw_lo, rb


def _shifted(u, prev, seq):
    b = u.shape[0] // seq
    ext = jnp.concatenate([prev.astype(u.dtype), u.reshape(b, seq, -1)], axis=1)
    w = u.shape[1]
    return ext[:, 1:-1].reshape(-1, w), ext[:, :-2].reshape(-1, w), ext[:, -(CONV_K - 1):]


def kernel(x_prompt, x_sample, state_conv, cache_mla_latent, cache_mla_rope, cache_moba_k, cache_moba_v,
           page_table, p_prompt, p_sample, w_in, conv_w, mla_q_norm, mla_kv_norm, mla_w_uq, mla_w_uk,
           mla_w_uv, w_o, ln1_g, ln1_b, router_w, router_b, w_gu, b_gu, w_down, b_down, ple_w,
           ple_gate_w, ple_gate_b, ln2_g, ln2_b):
    batch, seq, d = x_prompt.shape
    dec_batch, t_new, _ = x_sample.shape
    depth = w_in.shape[0]
    n_p, n_s = batch * seq, dec_batch * t_new
    n_pages = page_table.shape[1]
    past = n_pages * PAGE
    tm = min(TOK_TILE, n_s)
    tq = min(ATT_TILE, seq)
    assert n_p % tm == 0 and n_s % tm == 0 and seq % tm == 0 and tm % t_new == 0 and seq % MOBA_BLK == 0
    alpha = (2 * depth) ** 0.25
    cache_ropet = cache_mla_rope.transpose(0, 1, 3, 2)
    cache_kt = cache_moba_k.transpose(0, 1, 3, 4, 2).reshape(cache_moba_k.shape[:2] + (LANES, PAGE))
    cache_vt = cache_moba_v.transpose(0, 1, 3, 4, 2).reshape(cache_moba_v.shape[:2] + (LANES, PAGE))
    tab = _rope_table(seq, past, t_new, tm)
    x = jnp.concatenate([x_prompt.reshape(n_p, d), x_sample.reshape(n_s, d)], axis=0)
    outs = [[] for _ in range(10)]
    for i in range(depth):
        win, wuq, wuk, wuv, rw_hi, rw_lo, rb = _prep_layer(w_in[i], mla_w_uq[i], mla_w_uk[i], mla_w_uv[i],
                                                           router_w[i], router_b[i])
        u, cvb, qcat, kcat, ckv, kpe, mq, mk, mv, mkv = _proj(
            x, tab, win, wuq, wuk, mla_q_norm[i][None, :], mla_kv_norm[i][None, :], n_prompt=n_p, seq=seq, tm=tm)
        mla_p = _mla_prompt(qcat, kcat, wuv, batch=batch, seq=seq, tq=tq)
        moba_p = _moba_prompt(mq, mkv, mk, batch=batch, seq=seq)
        q_s = qcat[:, n_p:].reshape(MLA_H, dec_batch, t_new, MLA_KW).transpose(1, 0, 2, 3)
        q_s = q_s.reshape(dec_batch, MLA_H * t_new, MLA_KW)
        k_new = jnp.concatenate([ckv[n_p:], kpe[n_p:]], axis=1).reshape(dec_batch, t_new, MLA_KW)
        mla_s = _mla_sample(page_table, q_s, k_new, wuv, cache_mla_latent, cache_ropet, layer=i)
        mq_s = mq[n_p:].reshape(dec_batch, t_new, MOBA_KVH, MOBA_G, MOBA_D).transpose(0, 2, 3, 1, 4)
        mq_s = mq_s.reshape(dec_batch, MOBA_KVH, MOBA_G * t_new, MOBA_D)
        zero = jnp.zeros_like(mq_s[:, 0])
        q_bd = jnp.concatenate([jnp.concatenate([mq_s[:, 0], zero], axis=2),
                                jnp.concatenate([zero, mq_s[:, 1]], axis=2)], axis=1)
        moba_s = _moba_sample(page_table, q_bd, mk[n_p:].reshape(dec_batch, t_new, LANES),
                              mv[n_p:].reshape(dec_batch, t_new, LANES), cache_kt, cache_vt, layer=i)
        moba_s = moba_s.reshape(dec_batch, MOBA_KVH, MOBA_G, t_new, MOBA_KVH, MOBA_D)
        moba_s = jnp.stack([moba_s[:, g, :, :, g] for g in range(MOBA_KVH)], axis=1)
        moba_s = moba_s.transpose(0, 3, 1, 2, 4).reshape(n_s, MOBA_H * MOBA_D)
        mla_v = jnp.concatenate([mla_p, mla_s.reshape(n_s, -1).astype(BF16)], axis=0)
        moba_o = jnp.concatenate([moba_p, moba_s.astype(BF16)], axis=0)
        u1p, u2p, st_p = _shifted(u[:n_p], jnp.zeros((batch, CONV_K - 1, CONV_W), F32), seq)
        u1s, u2s, st_s = _shifted(u[n_p:], state_conv[i], t_new)
        u1 = jnp.concatenate([u1p, u1s], axis=0)
        u2 = jnp.concatenate([u2p, u2s], axis=0)
        p = jnp.concatenate([p_prompt[i].reshape(n_p, -1), p_sample[i].reshape(n_s, -1)], axis=0)
        h, ple, topi, topg, rank, cnt = _post1(
            x, u, u1, u2, cvb, conv_w[i], mla_v, moba_o, p, w_o[i].astype(BF16), ln1_g[i][None, :], ln1_b[i][None, :],
            rw_hi, rw_lo, rb, ple_gate_w[i].astype(BF16), ple_gate_b[i][None, :], ple_w[i].astype(BF16),
            alpha=alpha, tm=tm)
        block_e, row_tok, pos = _route(topi[:, :TOP_K], rank[:, :TOP_K], cnt[-1, 0, :N_EXP].astype(I32), MOE_TILE)
        y_rows = _moe(block_e, row_tok, h, w_gu[i].astype(BF16), b_gu[i][:, None, :],
                      w_down[i].astype(BF16), b_down[i][:, None, :], blk=MOE_TILE)
        tm2 = min(ATT_TILE, tm)
        x = _post2(pos.reshape(-1, 1, tm2 * TOP_K), h, ple, topg, ln2_g[i][None, :], ln2_b[i][None, :], y_rows,
                   alpha=alpha, tm=tm2)
        for lst, a, shp in ((outs[0], st_p, None), (outs[1], st_s, None),
                            (outs[2], ckv[:n_p], (batch, seq, MLA_R)), (outs[3], ckv[n_p:], (dec_batch, t_new, MLA_R)),
                            (outs[4], kpe[:n_p], (batch, seq, MLA_ROPE)), (outs[5], kpe[n_p:], (dec_batch, t_new, MLA_ROPE)),
                            (outs[6], mk[:n_p], (batch, seq, MOBA_KVH, MOBA_D)),
                            (outs[7], mk[n_p:], (dec_batch, t_new, MOBA_KVH, MOBA_D)),
                            (outs[8], mv[:n_p], (batch, seq, MOBA_KVH, MOBA_D)),
                            (outs[9], mv[n_p:], (dec_batch, t_new, MOBA_KVH, MOBA_D))):
            lst.append(a if shp is None else a.reshape(shp))
    y_p = x[:n_p].reshape(batch, seq, d)
    y_s = x[n_p:].reshape(dec_batch, t_new, d)
    return (y_p, y_s) + tuple(jnp.stack(o) for o in outs)
```

```python
import functools

import numpy as np
import jax
import jax.numpy as jnp
from jax import lax
from jax.experimental import pallas as pl
from jax.experimental.pallas import tpu as pltpu

F32 = jnp.float32
BF16 = jnp.bfloat16
I32 = jnp.int32

LANES = 128
SUBLANES = 8
PAGE = 128
CONV_W = 256
CONV_K = 3
MLA_H = 6
MLA_QL = 256
MLA_R = 128
MLA_NOPE = 64
MLA_ROPE = 32
MLA_V = 64
MLA_THETA = 10000.0
MLA_KW = MLA_R + MLA_ROPE
MOBA_H = 6
MOBA_KVH = 2
MOBA_G = MOBA_H // MOBA_KVH
MOBA_D = 64
MOBA_BLK = 256
MOBA_TOPK = 3
MOBA_ROT = MOBA_D // 4
MOBA_THETA = 500000.0
N_EXP = 32
TOP_K = 4
D_FF = 1024
SWIGLU_LIMIT = 7.0
SWIGLU_ALPHA = 1.702
LN_EPS = 1e-5
RMS_EPS = 1e-6
NEG = -1e30
MLA_SCALE = (MLA_NOPE + MLA_ROPE) ** -0.5
MOBA_SCALE = MOBA_D ** -0.5

OFF_H, OFF_B, OFF_C, OFF_QL, OFF_KV, OFF_MQ, OFF_MK, OFF_MV, OFF_KPE = 0, 256, 512, 768, 1024, 1152, 1536, 1664, 1792
IN_W = 1920
UQ_W = 640

TOK_TILE = 512
ATT_TILE = 256
MOE_TILE = 256
PAGES_PER_CHUNK = 16
VMEM_LIMIT = 56 * 1024 * 1024


def _nt(a, b):
    return lax.dot_general(a, b, (((1,), (1,)), ((), ())), preferred_element_type=F32)


def _dot(a, b):
    return jnp.dot(a, b, preferred_element_type=F32)


def _params(sem):
    return pltpu.CompilerParams(dimension_semantics=sem, vmem_limit_bytes=VMEM_LIMIT)


def _rope_lane_tables(pos, dim, theta, period):
    half = dim // 2
    inv = theta ** (-jnp.arange(0, dim, 2, dtype=F32) / dim)
    ang = pos.astype(F32)[:, None] * inv[None, :]
    cos, sin = jnp.cos(ang), jnp.sin(ang)
    d = np.arange(LANES) % period
    j = d % half
    rot = jnp.asarray(d < dim)[None, :]
    first = jnp.asarray(d < half)[None, :]
    c = jnp.where(rot, cos[:, j], 1.0)
    s1 = jnp.where(rot & first, -sin[:, j], 0.0)
    s2 = jnp.where(rot & ~first, sin[:, j], 0.0)
    return c, s1, s2


def _rope_table(seq, past, t_new, tile):
    pos = jnp.concatenate([jnp.arange(seq, dtype=I32), past + (jnp.arange(tile, dtype=I32) % t_new)])
    mla = _rope_lane_tables(pos, MLA_ROPE, MLA_THETA, MLA_ROPE)
    moba = _rope_lane_tables(pos, MOBA_ROT, MOBA_THETA, MOBA_D)
    return jnp.concatenate(list(mla) + list(moba), axis=1)


def _rope(x, tab, base, half):
    c = tab[:, base:base + LANES]
    s1 = tab[:, base + LANES:base + 2 * LANES]
    s2 = tab[:, base + 2 * LANES:base + 3 * LANES]
    return x * c + pltpu.roll(x, LANES - half, 1) * s1 + pltpu.roll(x, half, 1) * s2


def _proj_kernel(x_ref, tab_ref, win_ref, wuq_ref, wuk_ref, qn_ref, kvn_ref,
                 u_ref, cvb_ref, qcat_ref, kcat_ref, ckv_ref, kpe_ref, mq_ref, mk_ref, mv_ref, mkv_ref):
    y = _dot(x_ref[...].astype(BF16), win_ref[...])
    tab = tab_ref[...]
    u_ref[...] = y[:, OFF_C:OFF_C + CONV_W] * y[:, OFF_H:OFF_H + CONV_W]
    cvb_ref[...] = y[:, OFF_B:OFF_B + CONV_W]
    q_lat = y[:, OFF_QL:OFF_QL + MLA_QL]
    q_n = q_lat * lax.rsqrt(jnp.mean(q_lat * q_lat, axis=1, keepdims=True) + RMS_EPS) * qn_ref[...]
    q = _dot(q_n.astype(BF16), wuq_ref[...])
    n_nope = MLA_H * MLA_NOPE
    q_abs = _dot(q[:, :n_nope].astype(BF16), wuk_ref[...]) * MLA_SCALE
    q_pe = [_rope(q[:, n_nope + k * LANES:n_nope + (k + 1) * LANES], tab, 0, MLA_ROPE // 2) * MLA_SCALE
            for k in range(2)]
    per_slab = LANES // MLA_ROPE
    for h in range(MLA_H):
        qcat_ref[h, :, 0:MLA_R] = q_abs[:, h * MLA_R:(h + 1) * MLA_R].astype(BF16)
        lo = (h % per_slab) * MLA_ROPE
        qcat_ref[h, :, MLA_R:MLA_KW] = q_pe[h // per_slab][:, lo:lo + MLA_ROPE].astype(BF16)
    kv_lat = y[:, OFF_KV:OFF_KV + MLA_R]
    c_kv = kv_lat * lax.rsqrt(jnp.mean(kv_lat * kv_lat, axis=1, keepdims=True) + RMS_EPS) * kvn_ref[...]
    k_pe = _rope(y[:, OFF_KPE:OFF_KPE + LANES], tab, 0, MLA_ROPE // 2)[:, :MLA_ROPE]
    ckv_ref[...] = c_kv
    kpe_ref[...] = k_pe
    kcat_ref[:, 0:MLA_R] = c_kv.astype(BF16)
    kcat_ref[:, MLA_R:MLA_KW] = k_pe.astype(BF16)
    for k in range(MOBA_H * MOBA_D // LANES):
        slab = _rope(y[:, OFF_MQ + k * LANES:OFF_MQ + (k + 1) * LANES], tab, 3 * LANES, MOBA_ROT // 2)
        mq_ref[:, k * LANES:(k + 1) * LANES] = (slab * MOBA_SCALE).astype(BF16)
    mk = _rope(y[:, OFF_MK:OFF_MK + LANES], tab, 3 * LANES, MOBA_ROT // 2)
    mv = y[:, OFF_MV:OFF_MV + LANES]
    mk_ref[...] = mk
    mv_ref[...] = mv
    mkv_ref[:, 0:LANES] = mk.astype(BF16)
    mkv_ref[:, LANES:2 * LANES] = mv.astype(BF16)


def _proj(x, tab, win, wuq, wuk, qn, kvn, *, n_prompt, seq, tm):
    n = x.shape[0]
    n_pt = n_prompt // tm
    per_seq = seq // tm

    def tab_map(i):
        return (jnp.where(i < n_pt, i % per_seq, per_seq), 0)

    row = lambda w: pl.BlockSpec((tm, w), lambda i: (i, 0))
    full = lambda a: pl.BlockSpec(a.shape, lambda i: (0,) * a.ndim)
    out_shape = [
        jax.ShapeDtypeStruct((n, CONV_W), F32), jax.ShapeDtypeStruct((n, CONV_W), F32),
        jax.ShapeDtypeStruct((MLA_H, n, MLA_KW), BF16), jax.ShapeDtypeStruct((n, MLA_KW), BF16),
        jax.ShapeDtypeStruct((n, MLA_R), F32), jax.ShapeDtypeStruct((n, MLA_ROPE), F32),
        jax.ShapeDtypeStruct((n, MOBA_H * MOBA_D), BF16), jax.ShapeDtypeStruct((n, LANES), F32),
        jax.ShapeDtypeStruct((n, LANES), F32), jax.ShapeDtypeStruct((n, 2 * LANES), BF16),
    ]
    out_specs = [
        row(CONV_W), row(CONV_W), pl.BlockSpec((MLA_H, tm, MLA_KW), lambda i: (0, i, 0)), row(MLA_KW),
        row(MLA_R), row(MLA_ROPE), row(MOBA_H * MOBA_D), row(LANES), row(LANES), row(2 * LANES),
    ]
    return pl.pallas_call(
        _proj_kernel, grid=(n // tm,),
        in_specs=[row(x.shape[1]), pl.BlockSpec((tm, tab.shape[1]), tab_map),
                  full(win), full(wuq), full(wuk), full(qn), full(kvn)],
        out_specs=out_specs, out_shape=out_shape,
        compiler_params=_params(("arbitrary",)), name="proj",
    )(x, tab, win, wuq, wuk, qn, kvn)


def _softmax_update(carry, s, v):
    m, l, acc = carry
    m_new = jnp.maximum(m, jnp.max(s, axis=1, keepdims=True))
    p = jnp.exp(s - m_new)
    corr = jnp.exp(m - m_new)
    l = l * corr + jnp.sum(p, axis=1, keepdims=True)
    acc = acc * corr + _dot(p.astype(BF16), v)
    return m_new, l, acc


def _softmax_init(rows, width):
    return (jnp.full((rows, 1), NEG, F32), jnp.zeros((rows, 1), F32), jnp.zeros((rows, width), F32))


def _causal_mask(rows, t_q, cols):
    r = lax.broadcasted_iota(I32, (rows, cols), 0) % t_q
    c = lax.broadcasted_iota(I32, (rows, cols), 1)
    return c <= r


def _mla_prompt_kernel(q_ref, k_ref, wuv_ref, o_ref, *, tq):
    i = pl.program_id(1)
    rows = MLA_H * tq
    q = q_ref[...].reshape(rows, MLA_KW)

    def tile(j):
        k = k_ref[pl.ds(pl.multiple_of(j * tq, tq), tq), :]
        return _nt(q, k), k[:, :MLA_R]

    s, v = tile(i)
    s = jnp.where(_causal_mask(rows, tq, tq), s, NEG)
    carry = _softmax_update(_softmax_init(rows, MLA_R), s, v)
    carry = lax.fori_loop(0, i, lambda j, c: _softmax_update(c, *tile(j)), carry)
    _, l, acc = carry
    o = acc / l
    outs = [_dot(o[h * tq:(h + 1) * tq].astype(BF16), wuv_ref[h]) for h in range(MLA_H)]
    o_ref[...] = jnp.concatenate(outs, axis=1).astype(o_ref.dtype)


def _mla_prompt(qcat, kcat, wuv, *, batch, seq, tq):
    nq = seq // tq
    return pl.pallas_call(
        functools.partial(_mla_prompt_kernel, tq=tq), grid=(batch, nq),
        in_specs=[pl.BlockSpec((MLA_H, tq, MLA_KW), lambda b, i: (0, b * nq + i, 0)),
                  pl.BlockSpec((seq, MLA_KW), lambda b, i: (b, 0)),
                  pl.BlockSpec(wuv.shape, lambda b, i: (0, 0, 0))],
        out_specs=pl.BlockSpec((tq, MLA_H * MLA_V), lambda b, i: (b * nq + i, 0)),
        out_shape=jax.ShapeDtypeStruct((batch * seq, MLA_H * MLA_V), BF16),
        compiler_params=_params(("arbitrary", "arbitrary")), name="mla_prompt",
    )(qcat, kcat, wuv)


def _moba_prompt_kernel(q_ref, kv_ref, kf_ref, o_ref, km_ref, *, nb):
    i = pl.program_id(1)
    tq = MOBA_BLK
    rows = MOBA_G * tq

    @pl.when(i == 0)
    def _():
        km_ref[...] = jnp.zeros(km_ref.shape, F32)
        for n in range(nb):
            km_ref[n:n + 1, :] = jnp.mean(kf_ref[n * tq:(n + 1) * tq, :], axis=0, keepdims=True)

    q_all = q_ref[...]
    nbp = -(-nb // SUBLANES) * SUBLANES
    blk_i = lax.broadcasted_iota(I32, (nbp, rows), 0)
    blk_row = lax.broadcasted_iota(I32, (LANES, tq), 0)
    causal = _causal_mask(rows, tq, tq)
    pieces = []
    for g in range(MOBA_KVH):
        qg = jnp.concatenate([q_all[:, (MOBA_G * g + hh) * MOBA_D:(MOBA_G * g + hh + 1) * MOBA_D]
                              for hh in range(MOBA_G)], axis=0)
        gate = _nt(km_ref[:nbp, g * MOBA_D:(g + 1) * MOBA_D].astype(BF16), qg)
        gate = jnp.where(blk_i < i, gate, NEG)
        cnt = jnp.zeros((nbp, rows), I32)
        for m in range(nb - 1):
            row = gate[m:m + 1, :]
            beats = (row > gate) | ((row == gate) & (blk_i > m))
            cnt = cnt + beats.astype(I32)
        sel_t = jnp.where((blk_i < i) & (cnt < MOBA_TOPK), 1.0, 0.0)
        sel = jnp.concatenate([sel_t, jnp.zeros((LANES - nbp, rows), F32)], axis=0).T.astype(BF16)

        def kv_tile(n):
            t = kv_ref[pl.ds(pl.multiple_of(n * tq, tq), tq), :]
            return t[:, g * MOBA_D:(g + 1) * MOBA_D], t[:, LANES + g * MOBA_D:LANES + (g + 1) * MOBA_D]

        k, v = kv_tile(i)
        s = jnp.where(causal, _nt(qg, k), NEG)
        carry = _softmax_update(_softmax_init(rows, MOBA_D), s, v)

        def past(n, c):
            k, v = kv_tile(n)
            picked = _dot(sel, (blk_row == n).astype(BF16))
            return _softmax_update(c, jnp.where(picked > 0.5, _nt(qg, k), NEG), v)

        _, l, acc = lax.fori_loop(0, i, past, carry)
        o = acc / l
        pieces += [o[hh * tq:(hh + 1) * tq] for hh in range(MOBA_G)]
    o_ref[...] = jnp.concatenate(pieces, axis=1).astype(o_ref.dtype)


def _moba_prompt(mq, mkv, mk, *, batch, seq):
    nb = seq // MOBA_BLK
    return pl.pallas_call(
        functools.partial(_moba_prompt_kernel, nb=nb), grid=(batch, nb),
        in_specs=[pl.BlockSpec((MOBA_BLK, MOBA_H * MOBA_D), lambda b, i: (b * nb + i, 0)),
                  pl.BlockSpec((seq, 2 * LANES), lambda b, i: (b, 0)),
                  pl.BlockSpec((seq, LANES), lambda b, i: (b, 0))],
        out_specs=pl.BlockSpec((MOBA_BLK, MOBA_H * MOBA_D), lambda b, i: (b * nb + i, 0)),
        out_shape=jax.ShapeDtypeStruct((batch * seq, MOBA_H * MOBA_D), BF16),
        scratch_shapes=[pltpu.VMEM((LANES, LANES), F32)],
        compiler_params=_params(("arbitrary", "arbitrary")), name="moba_prompt",
    )(mq, mkv, mk)


RING_SLOTS = 4


def _ring_slots(steps):
    return max(n for n in range(1, RING_SLOTS + 1) if steps % n == 0)


def _mla_sample_kernel(pt_ref, q_ref, knew_ref, wuv_ref, lat_hbm, ropet_hbm, o_ref,
                       lat_buf, rope_buf, lat_sem, rope_sem, latb_ref, s_ref, *, layer, ppc, nchunk, nslot, t_new):
    b = pl.program_id(0)
    rows = MLA_H * t_new
    keys = ppc * PAGE

    def copies(bb, chunk, slot):
        out = []
        for j in range(ppc):
            pg = pt_ref[bb, chunk * ppc + j]
            out.append(pltpu.make_async_copy(lat_hbm.at[layer, pg], lat_buf.at[slot, j], lat_sem.at[slot, j]))
            out.append(pltpu.make_async_copy(ropet_hbm.at[layer, pg], rope_buf.at[slot, :, pl.ds(j * PAGE, PAGE)],
                                             rope_sem.at[slot, j]))
        return out

    def start(bb, chunk):
        for cp in copies(bb, chunk, chunk % nslot):
            cp.start()

    ahead = nslot - 1

    @pl.when(b == 0)
    def _():
        for c in range(ahead):
            start(b, c)

    q = q_ref[0]
    q_abs, q_pe = q[:, :MLA_R], q[:, MLA_R:]
    for c in range(nchunk):
        slot = c % nslot
        if c + ahead < nchunk:
            start(b, c + ahead)
        else:
            pl.when(b + 1 < pl.num_programs(0))(functools.partial(start, b + 1, c + ahead - nchunk))
        for cp in copies(b, c, slot):
            cp.wait()
        lat = lat_buf[slot].reshape(keys, MLA_R).astype(BF16)
        latb_ref[c * keys:(c + 1) * keys, :] = lat
        s_ref[:, c * keys:(c + 1) * keys] = _nt(q_abs, lat) + _dot(q_pe, rope_buf[slot].astype(BF16))

    k_new = knew_ref[0].astype(BF16)
    s_new = jnp.where(_causal_mask(rows, t_new, t_new), _nt(q, k_new), NEG)
    s_all = s_ref[...]
    m = jnp.maximum(jnp.max(s_all, axis=1, keepdims=True), jnp.max(s_new, axis=1, keepdims=True))
    p = jnp.exp(s_all - m)
    p_new = jnp.exp(s_new - m)
    l = jnp.sum(p, axis=1, keepdims=True) + jnp.sum(p_new, axis=1, keepdims=True)
    acc = _dot(p.astype(BF16), latb_ref[...]) + _dot(p_new.astype(BF16), k_new[:, :MLA_R])
    o = acc / l
    outs = [_dot(o[h * t_new:(h + 1) * t_new].astype(BF16), wuv_ref[h]) for h in range(MLA_H)]
    o_ref[0] = jnp.concatenate(outs, axis=1)


def _mla_sample(page_table, q, k_new, wuv, cache_lat, cache_ropet, *, layer):
    nbatch, n_pages = page_table.shape
    t_new = k_new.shape[1]
    ppc = min(PAGES_PER_CHUNK, n_pages)
    nchunk = n_pages // ppc
    assert n_pages % ppc == 0
    nslot = _ring_slots(nchunk)
    rows = MLA_H * t_new
    past = n_pages * PAGE
    grid_spec = pltpu.PrefetchScalarGridSpec(
        num_scalar_prefetch=1, grid=(nbatch,),
        in_specs=[pl.BlockSpec((1, rows, MLA_KW), lambda b, pt: (b, 0, 0)),
                  pl.BlockSpec((1, t_new, MLA_KW), lambda b, pt: (b, 0, 0)),
                  pl.BlockSpec(wuv.shape, lambda b, pt: (0, 0, 0)),
                  pl.BlockSpec(memory_space=pl.ANY), pl.BlockSpec(memory_space=pl.ANY)],
        out_specs=pl.BlockSpec((1, t_new, MLA_H * MLA_V), lambda b, pt: (b, 0, 0)),
        scratch_shapes=[pltpu.VMEM((nslot, ppc, PAGE, MLA_R), F32), pltpu.VMEM((nslot, MLA_ROPE, ppc * PAGE), F32),
                        pltpu.SemaphoreType.DMA((nslot, ppc)), pltpu.SemaphoreType.DMA((nslot, ppc)),
                        pltpu.VMEM((past, MLA_R), BF16), pltpu.VMEM((rows, past), F32)])
    return pl.pallas_call(
        functools.partial(_mla_sample_kernel, layer=layer, ppc=ppc, nchunk=nchunk, nslot=nslot, t_new=t_new),
        grid_spec=grid_spec, out_shape=jax.ShapeDtypeStruct((nbatch, t_new, MLA_H * MLA_V), F32),
        compiler_params=_params(("arbitrary",)), name="mla_sample",
    )(page_table, q, k_new, wuv, cache_lat, cache_ropet)


def _moba_sample_kernel(pt_ref, q_ref, knew_ref, vnew_ref, kt_hbm, vt_hbm, o_ref,
                        buf, sem, s_ref, p_ref, gate_ref, *, layer, ppc, nchunk, nslot, t_new, n_full):
    b = pl.program_id(0)
    rows = q_ref.shape[1]
    keys = ppc * PAGE
    bpc = keys // MOBA_BLK
    srcs = (kt_hbm, vt_hbm)

    def copies(bb, step, slot):
        src, chunk = srcs[step // nchunk], step % nchunk
        return [pltpu.make_async_copy(src.at[layer, pt_ref[bb, chunk * ppc + j]],
                                      buf.at[slot, :, pl.ds(j * PAGE, PAGE)], sem.at[slot, j]) for j in range(ppc)]

    nstep = 2 * nchunk
    ahead = nslot - 1

    def start(bb, step):
        for cp in copies(bb, step, step % nslot):
            cp.start()

    def fetch(step):
        slot = step % nslot
        if step + ahead < nstep:
            start(b, step + ahead)
        else:
            pl.when(b + 1 < pl.num_programs(0))(functools.partial(start, b + 1, step + ahead - nstep))
        for cp in copies(b, step, slot):
            cp.wait()
        return buf[slot].astype(BF16)

    @pl.when(b == 0)
    def _():
        for st in range(ahead):
            start(b, st)

    q = q_ref[0]
    lane = lax.broadcasted_iota(I32, (rows, LANES), 1)
    gate_ref[...] = jnp.full(gate_ref.shape, NEG, F32)
    for c in range(nchunk):
        s = _dot(q, fetch(c))
        s_ref[:, c * keys:(c + 1) * keys] = s
        for n in range(bpc):
            blk = c * bpc + n
            gate_ref[:, blk:blk + 1] = jnp.sum(s[:, n * MOBA_BLK:(n + 1) * MOBA_BLK], axis=1, keepdims=True)

    gate = jnp.where(lane < n_full, gate_ref[...], NEG)
    sel = jnp.zeros((rows, LANES), F32)
    for _ in range(MOBA_TOPK):
        best = jnp.max(gate, axis=1, keepdims=True)
        first = jnp.min(jnp.where(gate == best, lane, LANES), axis=1, keepdims=True)
        hit = lane == first
        sel = jnp.where(hit, 1.0, sel)
        gate = jnp.where(hit, NEG * 2, gate)
    sel = sel.astype(BF16)

    def picked(c):
        blk = lax.broadcasted_iota(I32, (LANES, keys), 0)
        key_blk = c * bpc + lax.broadcasted_iota(I32, (LANES, keys), 1) // MOBA_BLK
        return _dot(sel, (blk == key_blk).astype(BF16)) > 0.5

    new_mask = _causal_mask(rows, t_new, t_new)
    s_new = jnp.where(new_mask, _nt(q, knew_ref[0].astype(BF16)), NEG)
    m = jnp.max(s_new, axis=1, keepdims=True)
    for c in range(nchunk):
        m = jnp.maximum(m, jnp.max(jnp.where(picked(c), s_ref[:, c * keys:(c + 1) * keys], NEG), axis=1, keepdims=True))
    p_new = jnp.where(new_mask, jnp.exp(s_new - m), 0.0)
    l = jnp.sum(p_new, axis=1, keepdims=True)
    for c in range(nchunk):
        p = jnp.where(picked(c), jnp.exp(s_ref[:, c * keys:(c + 1) * keys] - m), 0.0)
        l = l + jnp.sum(p, axis=1, keepdims=True)
        p_ref[:, c * keys:(c + 1) * keys] = p.astype(BF16)

    acc = _dot(p_new.astype(BF16), vnew_ref[0].astype(BF16))
    for c in range(nchunk):
        acc = acc + _nt(p_ref[:, c * keys:(c + 1) * keys], fetch(nchunk + c))
    o_ref[0] = acc / l


def _moba_sample(page_table, q_bd, k_new, v_new, cache_kt, cache_vt, *, layer):
    nbatch, n_pages = page_table.shape
    t_new = k_new.shape[1]
    rows = q_bd.shape[1]
    ppc = min(PAGES_PER_CHUNK, n_pages)
    keys = ppc * PAGE
    past = n_pages * PAGE
    assert n_pages % ppc == 0 and keys % MOBA_BLK == 0 and past % MOBA_BLK == 0
    n_full = past // MOBA_BLK
    assert MOBA_TOPK <= n_full <= LANES
    nchunk = n_pages // ppc
    nslot = _ring_slots(2 * nchunk)
    grid_spec = pltpu.PrefetchScalarGridSpec(
        num_scalar_prefetch=1, grid=(nbatch,),
        in_specs=[pl.BlockSpec((1, rows, LANES), lambda b, pt: (b, 0, 0)),
                  pl.BlockSpec((1, t_new, LANES), lambda b, pt: (b, 0, 0)),
                  pl.BlockSpec((1, t_new, LANES), lambda b, pt: (b, 0, 0)),
                  pl.BlockSpec(memory_space=pl.ANY), pl.BlockSpec(memory_space=pl.ANY)],
        out_specs=pl.BlockSpec((1, rows, LANES), lambda b, pt: (b, 0, 0)),
        scratch_shapes=[pltpu.VMEM((nslot, LANES, keys), F32), pltpu.SemaphoreType.DMA((nslot, ppc)),
                        pltpu.VMEM((rows, past), F32), pltpu.VMEM((rows, past), BF16),
                        pltpu.VMEM((rows, LANES), F32)])
    return pl.pallas_call(
        functools.partial(_moba_sample_kernel, layer=layer, ppc=ppc, nchunk=nchunk, nslot=nslot, t_new=t_new,
                          n_full=n_full),
        grid_spec=grid_spec, out_shape=jax.ShapeDtypeStruct((nbatch, rows, LANES), F32),
        compiler_params=_params(("arbitrary",)), name="moba_sample",
    )(page_table, q_bd, k_new, v_new, cache_kt, cache_vt)


def _layer_norm(x, g, b):
    mu = jnp.mean(x, axis=1, keepdims=True)
    xc = x - mu
    var = jnp.mean(xc * xc, axis=1, keepdims=True)
    return xc * lax.rsqrt(var + LN_EPS) * g + b


def _post1_kernel(x_ref, u_ref, u1_ref, u2_ref, cvb_ref, cw_ref, mla_ref, moba_ref, p_ref,
                  wo_ref, g_ref, b_ref, rwh_ref, rwl_ref, rb_ref, gw_ref, gb_ref, pw_ref,
                  h_ref, ple_ref, topi_ref, topg_ref, rank_ref, cnt_ref, run_ref, *, alpha):
    cw = cw_ref[...]
    conv = cvb_ref[...] * (cw[0:1] * u2_ref[...] + cw[1:2] * u1_ref[...] + cw[2:3] * u_ref[...])
    mix_in = jnp.concatenate([conv.astype(BF16), mla_ref[...], moba_ref[...]], axis=1)
    h = _layer_norm(alpha * x_ref[...] + _dot(mix_in, wo_ref[...]), g_ref[...], b_ref[...])
    h_ref[...] = h
    h_hi = h.astype(BF16)
    h_lo = (h - h_hi.astype(F32)).astype(BF16)
    ple_ref[...] = jax.nn.sigmoid(_dot(h_hi, gw_ref[...]) + gb_ref[...]) * _dot(p_ref[...].astype(BF16), pw_ref[...])
    logits = _dot(h_hi, rwh_ref[...]) + (_dot(h_lo, rwh_ref[...]) + _dot(h_hi, rwl_ref[...])) + rb_ref[...]
    tm = logits.shape[0]
    lane = lax.broadcasted_iota(I32, logits.shape, 1)
    logits = jnp.where(lane < N_EXP, logits, NEG)
    topi = jnp.zeros(logits.shape, I32)
    topv = jnp.zeros(logits.shape, F32)
    chosen = jnp.zeros(logits.shape, F32)
    picks = []
    v0 = None
    den = jnp.zeros((tm, 1), F32)
    for k in range(TOP_K):
        best = jnp.max(logits, axis=1, keepdims=True)
        first = jnp.min(jnp.where(logits == best, lane, LANES), axis=1, keepdims=True)
        v0 = best if v0 is None else v0
        e = jnp.exp(best - v0)
        den = den + e
        hit = lane == first
        picks.append(hit)
        chosen = jnp.where(hit, 1.0, chosen)
        topi = jnp.where(lane == k, first, topi)
        topv = jnp.where(lane == k, e, topv)
        logits = jnp.where(hit, NEG * 2, logits)
    topi_ref[...] = topi
    topg_ref[...] = topv / den

    @pl.when(pl.program_id(0) == 0)
    def _():
        run_ref[...] = jnp.zeros(run_ref.shape, F32)

    tri = lax.broadcasted_iota(I32, (tm, tm), 0) > lax.broadcasted_iota(I32, (tm, tm), 1)
    before = _dot(tri.astype(BF16), chosen.astype(BF16)) + run_ref[0:1, :]
    rank = jnp.zeros(logits.shape, I32)
    for k in range(TOP_K):
        r_k = jnp.sum(jnp.where(picks[k], before, 0.0), axis=1, keepdims=True)
        rank = jnp.where(lane == k, r_k.astype(I32), rank)
    rank_ref[...] = rank
    run = run_ref[...] + jnp.sum(chosen, axis=0, keepdims=True)
    run_ref[...] = run
    cnt_ref[0] = run


def _post1(x, u, u1, u2, cvb, cw, mla_v, moba_o, p, wo, g, b, rwh, rwl, rb, gw, gb, pw, *, alpha, tm):
    n, d = x.shape
    nt = n // tm
    row = lambda a: pl.BlockSpec((tm, a.shape[1]), lambda i: (i, 0))
    full = lambda a: pl.BlockSpec(a.shape, lambda i: (0,) * a.ndim)
    ins = (x, u, u1, u2, cvb, cw, mla_v, moba_o, p, wo, g, b, rwh, rwl, rb, gw, gb, pw)
    is_row = (1, 1, 1, 1, 1, 0, 1, 1, 1, 0, 0, 0, 0, 0, 0, 0, 0, 0)
    out_shape = [jax.ShapeDtypeStruct((n, d), F32), jax.ShapeDtypeStruct((n, d), F32),
                 jax.ShapeDtypeStruct((n, LANES), I32), jax.ShapeDtypeStruct((n, LANES), F32),
                 jax.ShapeDtypeStruct((n, LANES), I32), jax.ShapeDtypeStruct((nt, SUBLANES, LANES), F32)]
    out_specs = [pl.BlockSpec((tm, s.shape[1]), lambda i: (i, 0)) for s in out_shape[:5]]
    out_specs.append(pl.BlockSpec((1, SUBLANES, LANES), lambda i: (i, 0, 0)))
    return pl.pallas_call(
        functools.partial(_post1_kernel, alpha=alpha), grid=(nt,),
        in_specs=[row(a) if r else full(a) for a, r in zip(ins, is_row)],
        out_specs=out_specs, out_shape=out_shape,
        scratch_shapes=[pltpu.VMEM((SUBLANES, LANES), F32)],
        compiler_params=_params(("arbitrary",)), name="post1",
    )(*ins)


def _moe_kernel(be_ref, tok_ref, tokn_ref, h_hbm, wgu_ref, bgu_ref, wdn_ref, bdn_ref, y_ref, xbuf, sem, *, blk):
    i = pl.program_id(0)
    last = pl.num_programs(0) - 1
    slot = i % 2

    def gather(tok, sl):
        for r in range(blk):
            pltpu.make_async_copy(h_hbm.at[pl.ds(tok[0, 0, r], 1)], xbuf.at[sl, pl.ds(r, 1)], sem.at[sl]).start()

    def drain(sl):
        pltpu.make_async_copy(h_hbm.at[pl.ds(0, blk)], xbuf.at[sl], sem.at[sl]).wait()

    @pl.when(i == 0)
    def _():
        gather(tok_ref, slot)

    gather(tokn_ref, 1 - slot)
    drain(slot)
    x = xbuf[slot].astype(BF16)
    hgu = _dot(x, wgu_ref[0]) + bgu_ref[0]
    g = jnp.minimum(hgu[:, :D_FF], SWIGLU_LIMIT)
    u = jnp.clip(hgu[:, D_FF:], -SWIGLU_LIMIT, SWIGLU_LIMIT)
    act = (u + 1.0) * (g * jax.nn.sigmoid(SWIGLU_ALPHA * g))
    y_ref[...] = _dot(act.astype(BF16), wdn_ref[0]) + bdn_ref[0]

    @pl.when(i == last)
    def _():
        drain(1 - slot)


def _moe(block_e, row_tok, h, wgu, bgu, wdn, bdn, *, blk):
    n_blocks = row_tok.shape[0]
    d = h.shape[1]
    grid_spec = pltpu.PrefetchScalarGridSpec(
        num_scalar_prefetch=1, grid=(n_blocks,),
        in_specs=[pl.BlockSpec((1, 1, blk), lambda i, be: (i, 0, 0), memory_space=pltpu.SMEM),
                  pl.BlockSpec((1, 1, blk), lambda i, be: (jnp.minimum(i + 1, n_blocks - 1), 0, 0),
                               memory_space=pltpu.SMEM),
                  pl.BlockSpec(memory_space=pl.ANY),
                  pl.BlockSpec((1,) + wgu.shape[1:], lambda i, be: (be[i], 0, 0)),
                  pl.BlockSpec((1,) + bgu.shape[1:], lambda i, be: (be[i], 0, 0)),
                  pl.BlockSpec((1,) + wdn.shape[1:], lambda i, be: (be[i], 0, 0)),
                  pl.BlockSpec((1,) + bdn.shape[1:], lambda i, be: (be[i], 0, 0))],
        out_specs=pl.BlockSpec((blk, d), lambda i, be: (i, 0)),
        scratch_shapes=[pltpu.VMEM((2, blk, d), F32), pltpu.SemaphoreType.DMA((2,))])
    return pl.pallas_call(
        functools.partial(_moe_kernel, blk=blk), grid_spec=grid_spec,
        out_shape=jax.ShapeDtypeStruct((n_blocks * blk, d), F32),
        compiler_params=_params(("arbitrary",)), name="moe",
    )(block_e, row_tok, row_tok, h, wgu, bgu, wdn, bdn)


def _post2_kernel(pos_ref, posn_ref, h_ref, ple_ref, gate_ref, g_ref, b_ref, y_hbm, o_ref, ybuf, sem,
                  *, alpha, tm):
    i = pl.program_id(0)
    last = pl.num_programs(0) - 1
    slot = i % 2

    def gather(pos, sl):
        for r in range(tm):
            for k in range(TOP_K):
                pltpu.make_async_copy(y_hbm.at[pl.ds(pos[0, 0, r * TOP_K + k], 1)],
                                      ybuf.at[sl, k, pl.ds(r, 1)], sem.at[sl]).start(priority=k % 2)

    def drain(sl):
        for k in range(TOP_K):
            pltpu.make_async_copy(y_hbm.at[pl.ds(0, tm)], ybuf.at[sl, k], sem.at[sl]).wait()

    @pl.when(i == 0)
    def _():
        gather(pos_ref, slot)

    gather(posn_ref, 1 - slot)
    drain(slot)
    gate = gate_ref[...]
    ffn = gate[:, 0:1] * ybuf[slot, 0]
    for k in range(1, TOP_K):
        ffn = ffn + gate[:, k:k + 1] * ybuf[slot, k]
    o_ref[...] = _layer_norm(alpha * h_ref[...] + ffn + ple_ref[...], g_ref[...], b_ref[...])

    @pl.when(i == last)
    def _():
        drain(1 - slot)


def _post2(pos, h, ple, gate, g, b, y_rows, *, alpha, tm):
    n, d = h.shape
    nt = n // tm
    row = lambda a: pl.BlockSpec((tm, a.shape[1]), lambda i: (i, 0))
    full = lambda a: pl.BlockSpec(a.shape, lambda i: (0,) * a.ndim)
    return pl.pallas_call(
        functools.partial(_post2_kernel, alpha=alpha, tm=tm), grid=(nt,),
        in_specs=[pl.BlockSpec((1, 1, tm * TOP_K), lambda i: (i, 0, 0), memory_space=pltpu.SMEM),
                  pl.BlockSpec((1, 1, tm * TOP_K), lambda i: (jnp.minimum(i + 1, nt - 1), 0, 0),
                               memory_space=pltpu.SMEM),
                  row(h), row(ple), row(gate), full(g), full(b), pl.BlockSpec(memory_space=pl.ANY)],
        out_specs=pl.BlockSpec((tm, d), lambda i: (i, 0)),
        out_shape=jax.ShapeDtypeStruct((n, d), F32),
        scratch_shapes=[pltpu.VMEM((2, TOP_K, tm, d), F32), pltpu.SemaphoreType.DMA((2,))],
        compiler_params=_params(("arbitrary",)), name="post2",
    )(pos, pos, h, ple, gate, g, b, y_rows)


def _route(top_e, rank, counts, blk):
    n_tok = top_e.shape[0]
    n_asg = n_tok * TOP_K
    starts = jnp.cumsum(counts) - counts
    padded = (counts + blk - 1) // blk * blk
    pad_ends = jnp.cumsum(padded)
    pad_start = pad_ends - padded
    pos = (pad_start[top_e] + rank).reshape(n_asg).astype(I32)
    n_blocks = -(-n_asg // blk) + N_EXP
    n_used = pad_ends[-1] // blk
    blk_start = jnp.arange(n_blocks, dtype=I32) * blk
    block_e = jnp.minimum(jnp.sum((blk_start[:, None] >= pad_ends[None, :]).astype(I32), axis=1), N_EXP - 1)
    order = jnp.argsort(top_e.reshape(n_asg), stable=True)
    e_r = jnp.repeat(block_e, blk)
    off = jnp.arange(n_blocks * blk, dtype=I32) - pad_start[e_r]
    src = jnp.clip(starts[e_r] + off, 0, n_asg - 1)
    row_tok = jnp.where(off < counts[e_r], order[src] // TOP_K, 0).astype(I32)
    last_e = block_e[jnp.maximum(n_used - 1, 0)]
    block_e = jnp.where(jnp.arange(n_blocks) < n_used, block_e, last_e).astype(I32)
    return block_e, row_tok.reshape(n_blocks, 1, blk), pos


def _prep_layer(w_in, mla_w_uq, mla_w_uk, mla_w_uv, router_w, router_b):
    d = w_in.shape[0]
    cols = np.concatenate([np.arange(0, 1152), np.arange(1184, 1824), np.arange(1152, 1184)])
    win = jnp.concatenate([w_in[:, cols], jnp.zeros((d, IN_W - cols.size), w_in.dtype)], axis=1).astype(BF16)
    per = MLA_NOPE + MLA_ROPE
    nope = np.concatenate([np.arange(h * per, h * per + MLA_NOPE) for h in range(MLA_H)])
    rope = np.concatenate([np.arange(h * per + MLA_NOPE, (h + 1) * per) for h in range(MLA_H)])
    ucols = np.concatenate([nope, rope])
    wuq = jnp.concatenate([mla_w_uq[:, ucols], jnp.zeros((MLA_QL, UQ_W - ucols.size), mla_w_uq.dtype)],
                          axis=1).astype(BF16)
    wuk = jnp.zeros((MLA_H * MLA_NOPE, MLA_H * MLA_R), F32)
    for h in range(MLA_H):
        wuk = wuk.at[h * MLA_NOPE:(h + 1) * MLA_NOPE, h * MLA_R:(h + 1) * MLA_R].set(mla_w_uk[:, h, :].T)
    wuv = jnp.transpose(mla_w_uv, (1, 0, 2)).astype(BF16)
    rw = jnp.concatenate([router_w, jnp.zeros((d, LANES - N_EXP), router_w.dtype)], axis=1)
    rw_hi = rw.astype(BF16)
    rw_lo = (rw - rw_hi.astype(F32)).astype(BF16)
    rb = jnp.concatenate([router_b, jnp.zeros((LANES - N_EXP,), router_b.dtype)])[None, :]
    return win, wuq, wuk.astype(BF16), wuv, rw_hi, rw_lo, rb


def _shifted(u, prev, seq):
    b = u.shape[0] // seq
    ext = jnp.concatenate([prev.astype(u.dtype), u.reshape(b, seq, -1)], axis=1)
    w = u.shape[1]
    return ext[:, 1:-1].reshape(-1, w), ext[:, :-2].reshape(-1, w), ext[:, -(CONV_K - 1):]


def kernel(x_prompt, x_sample, state_conv, cache_mla_latent, cache_mla_rope, cache_moba_k, cache_moba_v,
           page_table, p_prompt, p_sample, w_in, conv_w, mla_q_norm, mla_kv_norm, mla_w_uq, mla_w_uk,
           mla_w_uv, w_o, ln1_g, ln1_b, router_w, router_b, w_gu, b_gu, w_down, b_down, ple_w,
           ple_gate_w, ple_gate_b, ln2_g, ln2_b):
    batch, seq, d = x_prompt.shape
    dec_batch, t_new, _ = x_sample.shape
    depth = w_in.shape[0]
    n_p, n_s = batch * seq, dec_batch * t_new
    n_pages = page_table.shape[1]
    past = n_pages * PAGE
    tm = min(TOK_TILE, n_s)
    tq = min(ATT_TILE, seq)
    assert n_p % tm == 0 and n_s % tm == 0 and seq % tm == 0 and tm % t_new == 0 and seq % MOBA_BLK == 0
    alpha = (2 * depth) ** 0.25
    cache_ropet = cache_mla_rope.transpose(0, 1, 3, 2)
    cache_kt = cache_moba_k.transpose(0, 1, 3, 4, 2).reshape(cache_moba_k.shape[:2] + (LANES, PAGE))
    cache_vt = cache_moba_v.transpose(0, 1, 3, 4, 2).reshape(cache_moba_v.shape[:2] + (LANES, PAGE))
    tab = _rope_table(seq, past, t_new, tm)
    x = jnp.concatenate([x_prompt.reshape(n_p, d), x_sample.reshape(n_s, d)], axis=0)
    outs = [[] for _ in range(10)]
    for i in range(depth):
        win, wuq, wuk, wuv, rw_hi, rw_lo, rb = _prep_layer(w_in[i], mla_w_uq[i], mla_w_uk[i], mla_w_uv[i],
                                                           router_w[i], router_b[i])
        u, cvb, qcat, kcat, ckv, kpe, mq, mk, mv, mkv = _proj(
            x, tab, win, wuq, wuk, mla_q_norm[i][None, :], mla_kv_norm[i][None, :], n_prompt=n_p, seq=seq, tm=tm)
        mla_p = _mla_prompt(qcat, kcat, wuv, batch=batch, seq=seq, tq=tq)
        moba_p = _moba_prompt(mq, mkv, mk, batch=batch, seq=seq)
        q_s = qcat[:, n_p:].reshape(MLA_H, dec_batch, t_new, MLA_KW).transpose(1, 0, 2, 3)
        q_s = q_s.reshape(dec_batch, MLA_H * t_new, MLA_KW)
        k_new = jnp.concatenate([ckv[n_p:], kpe[n_p:]], axis=1).reshape(dec_batch, t_new, MLA_KW)
        mla_s = _mla_sample(page_table, q_s, k_new, wuv, cache_mla_latent, cache_ropet, layer=i)
        mq_s = mq[n_p:].reshape(dec_batch, t_new, MOBA_KVH, MOBA_G, MOBA_D).transpose(0, 2, 3, 1, 4)
        mq_s = mq_s.reshape(dec_batch, MOBA_KVH, MOBA_G * t_new, MOBA_D)
        zero = jnp.zeros_like(mq_s[:, 0])
        q_bd = jnp.concatenate([jnp.concatenate([mq_s[:, 0], zero], axis=2),
                                jnp.concatenate([zero, mq_s[:, 1]], axis=2)], axis=1)
        moba_s = _moba_sample(page_table, q_bd, mk[n_p:].reshape(dec_batch, t_new, LANES),
                              mv[n_p:].reshape(dec_batch, t_new, LANES), cache_kt, cache_vt, layer=i)
        moba_s = moba_s.reshape(dec_batch, MOBA_KVH, MOBA_G, t_new, MOBA_KVH, MOBA_D)
        moba_s = jnp.stack([moba_s[:, g, :, :, g] for g in range(MOBA_KVH)], axis=1)
        moba_s = moba_s.transpose(0, 3, 1, 2, 4).reshape(n_s, MOBA_H * MOBA_D)
        mla_v = jnp.concatenate([mla_p, mla_s.reshape(n_s, -1).astype(BF16)], axis=0)
        moba_o = jnp.concatenate([moba_p, moba_s.astype(BF16)], axis=0)
        u1p, u2p, st_p = _shifted(u[:n_p], jnp.zeros((batch, CONV_K - 1, CONV_W), F32), seq)
        u1s, u2s, st_s = _shifted(u[n_p:], state_conv[i], t_new)
        u1 = jnp.concatenate([u1p, u1s], axis=0)
        u2 = jnp.concatenate([u2p, u2s], axis=0)
        p = jnp.concatenate([p_prompt[i].reshape(n_p, -1), p_sample[i].reshape(n_s, -1)], axis=0)
        h, ple, topi, topg, rank, cnt = _post1(
            x, u, u1, u2, cvb, conv_w[i], mla_v, moba_o, p, w_o[i].astype(BF16), ln1_g[i][None, :], ln1_b[i][None, :],
            rw_hi, rw_lo, rb, ple_gate_w[i].astype(BF16), ple_gate_b[i][None, :], ple_w[i].astype(BF16),
            alpha=alpha, tm=tm)
        block_e, row_tok, pos = _route(topi[:, :TOP_K], rank[:, :TOP_K], cnt[-1, 0, :N_EXP].astype(I32), MOE_TILE)
        y_rows = _moe(block_e, row_tok, h, w_gu[i].astype(BF16), b_gu[i][:, None, :],
                      w_down[i].astype(BF16), b_down[i][:, None, :], blk=MOE_TILE)
        tm2 = min(ATT_TILE, tm)
        x = _post2(pos.reshape(-1, 1, tm2 * TOP_K), h, ple, topg, ln2_g[i][None, :], ln2_b[i][None, :], y_rows,
                   alpha=alpha, tm=tm2)
        for lst, a, shp in ((outs[0], st_p, None), (outs[1], st_s, None),
                            (outs[2], ckv[:n_p], (batch, seq, MLA_R)), (outs[3], ckv[n_p:], (dec_batch, t_new, MLA_R)),
                            (outs[4], kpe[:n_p], (batch, seq, MLA_ROPE)), (outs[5], kpe[n_p:], (dec_batch, t_new, MLA_ROPE)),
                            (outs[6], mk[:n_p], (batch, seq, MOBA_KVH, MOBA_D)),
                            (outs[7], mk[n_p:], (dec_batch, t_new, MOBA_KVH, MOBA_D)),
                            (outs[8], mv[:n_p], (batch, seq, MOBA_KVH, MOBA_D)),
                            (outs[9], mv[n_p:], (dec_batch, t_new, MOBA_KVH, MOBA_D))):
            lst.append(a if shp is None else a.reshape(shp))
    y_p = x[:n_p].reshape(batch, seq, d)
    y_s = x[n_p:].reshape(dec_batch, t_new, d)
    return (y_p, y_s) + tuple(jnp.stack(o) for o in outs)
```
